```python
import math
import jax
import jax.numpy as jnp
from jax import lax
import numpy as np

D_MODEL = 1024
BATCH = 2
SEQ = 8192
DEPTH = 2
DEC_BATCH = 128
DEC_SEQ = 4
PAST_LEN = 16384
PAGE_SIZE = 128

MLA_HEADS = 8
MLA_Q_RANK = 384
MLA_KV_RANK = 256
MLA_D_NOPE = 64
MLA_D_ROPE = 32
MLA_D_V = 64
ROPE_THETA = 10000.0
SC_CHANNELS = 512
SC_WIDTH = 3
DIFF_HEADS = 8
DIFF_KV_HEADS = 2
DIFF_D = 32
D_FF = 2816
FFN_CONV_WIDTH = 3
REL_BUCKETS = 32
REL_MAX_DIST = 128
Q_BLOCK = 128
EPS = 1e-6
NEG_INF = -1e30

IN_SIZES = (MLA_Q_RANK, MLA_KV_RANK, MLA_D_ROPE,
            SC_CHANNELS, SC_CHANNELS, SC_CHANNELS,
            DIFF_HEADS * 2 * DIFF_D, DIFF_KV_HEADS * 2 * DIFF_D, DIFF_KV_HEADS * 2 * DIFF_D,
            D_MODEL, D_MODEL, D_MODEL)
D_IN = sum(IN_SIZES)

kernel_name = "hybrid_mla_shortconv_diffattn_step"


def rmsnorm(x, g):
    xf = x.astype(jnp.float32)
    y = xf * lax.rsqrt(jnp.mean(xf * xf, axis=-1, keepdims=True) + EPS)
    return (y * g.astype(jnp.float32)).astype(x.dtype)


def rope(x, pos):
    half = x.shape[-1] // 2
    inv = ROPE_THETA ** (-jnp.arange(half, dtype=jnp.float32) / half)
    ang = pos.astype(jnp.float32)[:, None] * inv[None, :]
    cos = jnp.cos(ang)[None, :, None, :]
    sin = jnp.sin(ang)[None, :, None, :]
    xf = x.astype(jnp.float32)
    x1, x2 = xf[..., :half], xf[..., half:]
    return jnp.concatenate([x1 * cos - x2 * sin, x1 * sin + x2 * cos], axis=-1).astype(x.dtype)


def rel_bucket(dist):
    n = jnp.maximum(dist, 0)
    max_exact = REL_BUCKETS // 2
    large = max_exact + (jnp.log(jnp.maximum(n, 1).astype(jnp.float32) / max_exact)
                         / math.log(REL_MAX_DIST / max_exact)
                         * (REL_BUCKETS - max_exact)).astype(jnp.int32)
    large = jnp.minimum(large, REL_BUCKETS - 1)
    return jnp.where(n < max_exact, n, large)


def sweep_query_blocks(block_fn, q_arrays):
    q_len = q_arrays[0].shape[1]
    blk = Q_BLOCK if q_len % Q_BLOCK == 0 else q_len
    nb = q_len // blk

    def split(a):
        return jnp.moveaxis(a.reshape(a.shape[0], nb, blk, *a.shape[2:]), 1, 0)

    starts = jnp.arange(nb, dtype=jnp.int32) * blk
    out = lax.map(lambda args: block_fn(args[0], args[1]),
                  (tuple(split(a) for a in q_arrays), starts))
    out = jnp.moveaxis(out, 0, 1)
    return out.reshape(out.shape[0], q_len, *out.shape[3:])


def mla_attention(q_lat, q_rope, ckv, krope, q_pos0):
    t_len = ckv.shape[1]
    k_pos = jnp.arange(t_len, dtype=jnp.int32)
    scale = (MLA_D_NOPE + MLA_D_ROPE) ** -0.5
    ckv_f = ckv.astype(jnp.float32)
    krope_f = krope.astype(jnp.float32)

    def block(qs, start):
        ql, qr = qs
        blk = ql.shape[1]
        q_pos = q_pos0 + start + jnp.arange(blk, dtype=jnp.int32)
        s = (jnp.einsum('bqhc,btc->bhqt', ql.astype(jnp.float32), ckv_f)
             + jnp.einsum('bqhr,btr->bhqt', qr.astype(jnp.float32), krope_f)) * scale
        mask = (k_pos[None, :] <= q_pos[:, None])[None, None]
        p = jax.nn.softmax(jnp.where(mask, s, NEG_INF), axis=-1)
        return jnp.einsum('bhqt,btc->bqhc', p, ckv_f)

    return sweep_query_blocks(block, (q_lat, q_rope))


def diff_attention(q, k, v, lam, rel_bias, q_pos0):
    t_len, n_groups = k.shape[1], k.shape[2]
    rep = DIFF_HEADS // n_groups
    k_pos = jnp.arange(t_len, dtype=jnp.int32)
    scale = DIFF_D ** -0.5
    kf = k.astype(jnp.float32)
    k1, k2 = kf[..., :DIFF_D], kf[..., DIFF_D:]
    vf = v.astype(jnp.float32)
    table = rel_bias.astype(jnp.float32)

    def block(qs, start):
        (qb,) = qs
        b, blk = qb.shape[0], qb.shape[1]
        q_pos = q_pos0 + start + jnp.arange(blk, dtype=jnp.int32)
        qf = qb.astype(jnp.float32).reshape(b, blk, n_groups, rep, 2 * DIFF_D)
        bucket = rel_bucket(q_pos[:, None] - k_pos[None, :])
        bias = table[bucket].reshape(blk, t_len, n_groups, rep).transpose(2, 3, 0, 1)
        mask = k_pos[None, :] <= q_pos[:, None]

        def softmax_map(qc, kc):
            s = jnp.einsum('bqgrd,btgd->bgrqt', qc, kc) * scale + bias[None]
            return jax.nn.softmax(jnp.where(mask, s, NEG_INF), axis=-1)

        a = softmax_map(qf[..., :DIFF_D], k1) - lam * softmax_map(qf[..., DIFF_D:], k2)
        o = jnp.einsum('bgrqt,btge->bqgre', a, vf)
        return o.reshape(b, blk, DIFF_HEADS, 2 * DIFF_D)

    return sweep_query_blocks(block, (q,))


def causal_dwconv(u, prev, w):
    ext = jnp.concatenate([prev.astype(u.dtype), u], axis=1)
    s_len, width = u.shape[1], w.shape[0]
    y = sum(ext[:, j:j + s_len] * w[j] for j in range(width))
    return y, ext[:, ext.shape[1] - (width - 1):]


def _layer(x, c, pos0, ckv_past, krope_past, dk_past, dv_past, sc_prev, ffn_prev, rel_bias, lp, lam_init):
    b, s_len, _ = x.shape
    dt = x.dtype
    mod = jnp.einsum('bd,de->be', jax.nn.silu(c), lp['w_ada']) + lp['b_ada']
    sh1, sc1, g1, sh2, sc2, g2 = jnp.split(mod[:, None, :].astype(dt), 6, axis=-1)

    h = rmsnorm(x, lp['g_norm1']) * (1 + sc1) + sh1
    z = jnp.einsum('bsd,de->bse', h, lp['w_in'])
    splits = np.cumsum(IN_SIZES)[:-1].tolist()
    zq, zkv, zkr, zb, zc, zx, zdq, zdk, zdv, ga, gb, gc = jnp.split(z, splits, axis=-1)
    pos = pos0 + jnp.arange(s_len, dtype=jnp.int32)

    cq = rmsnorm(zq, lp['g_cq'])
    ckv_new = rmsnorm(zkv, lp['g_ckv'])
    q = jnp.einsum('bsr,rhe->bshe', cq, lp['w_uq'])
    q_rope = rope(q[..., MLA_D_NOPE:], pos)
    q_lat = jnp.einsum('bshn,chn->bshc', q[..., :MLA_D_NOPE], lp['w_uk'])
    krope_new = rope(zkr[:, :, None, :], pos)[:, :, 0, :]
    o_lat = mla_attention(q_lat, q_rope,
                          jnp.concatenate([ckv_past.astype(dt), ckv_new], axis=1),
                          jnp.concatenate([krope_past.astype(dt), krope_new], axis=1), pos0)
    o_a = jnp.einsum('bshc,chv->bshv', o_lat, lp['w_uv']).reshape(b, s_len, MLA_HEADS * MLA_D_V)
    br_a = o_a.astype(dt) @ lp['w_oa']

    y_sc, sc_state = causal_dwconv(zc * zx, sc_prev, lp['w_sc'])
    br_b = (zb * y_sc) @ lp['w_ob']

    dq = zdq.reshape(b, s_len, DIFF_HEADS, 2 * DIFF_D)
    dk_new = zdk.reshape(b, s_len, DIFF_KV_HEADS, 2 * DIFF_D)
    dv_new = zdv.reshape(b, s_len, DIFF_KV_HEADS, 2 * DIFF_D)
    f32 = jnp.float32
    lam = (jnp.exp(jnp.sum(lp['lam_q1'].astype(f32) * lp['lam_k1'].astype(f32)))
           - jnp.exp(jnp.sum(lp['lam_q2'].astype(f32) * lp['lam_k2'].astype(f32))) + lam_init)
    o_c = diff_attention(dq,
                         jnp.concatenate([dk_past.astype(dt), dk_new], axis=1),
                         jnp.concatenate([dv_past.astype(dt), dv_new], axis=1),
                         lam, rel_bias, pos0)
    o_c = rmsnorm(o_c, lp['g_subln']) * (1.0 - lam_init)
    br_c = o_c.reshape(b, s_len, DIFF_HEADS * 2 * DIFF_D).astype(dt) @ lp['w_oc']

    merged = jax.nn.sigmoid(ga) * br_a + jax.nn.sigmoid(gb) * br_b + jax.nn.sigmoid(gc) * br_c
    x = x + g1 * (merged @ lp['w_o'])

    h2 = rmsnorm(x, lp['g_norm2']) * (1 + sc2) + sh2
    act, ffn_state = causal_dwconv(h2 @ lp['w_gate'], ffn_prev, lp['w_ffn_conv'])
    x = x + g2 * ((jax.nn.silu(act) * (h2 @ lp['w_up'])) @ lp['w_down'])
    return x, (ckv_new, krope_new, dk_new, dv_new, sc_state, ffn_state)


def setup_inputs(seed: int = 0) -> dict:
    key = jax.random.key(seed)
    keys = jax.random.split(key, 64)
    counter = [0]

    def nxt():
        counter[0] += 1
        return keys[counter[0] - 1]

    def nrm(shape, s):
        return jax.random.normal(nxt(), shape, jnp.float32) * s

    def gain(shape):
        return 1.0 + 0.02 * jax.random.normal(nxt(), shape, jnp.float32)

    n_pages = PAST_LEN // PAGE_SIZE
    n_pool = (DEC_BATCH * n_pages * 5) // 4
    perm = jax.random.permutation(nxt(), n_pool)
    page_table = perm[:DEC_BATCH * n_pages].reshape(DEC_BATCH, n_pages).astype(jnp.int32)
    L = DEPTH
    return {
        'x_prompt': nrm((BATCH, SEQ, D_MODEL), 1.0),
        'x_sample': nrm((DEC_BATCH, DEC_SEQ, D_MODEL), 1.0),
        'c_prompt': nrm((BATCH, D_MODEL), 1.0),
        'c_sample': nrm((DEC_BATCH, D_MODEL), 1.0),
        'cache_mla_ckv': nrm((L, n_pool, PAGE_SIZE, MLA_KV_RANK), 1.0),
        'cache_mla_krope': nrm((L, n_pool, PAGE_SIZE, MLA_D_ROPE), 1.0),
        'cache_diff_k': nrm((L, n_pool, PAGE_SIZE, DIFF_KV_HEADS, 2 * DIFF_D), 1.0),
        'cache_diff_v': nrm((L, n_pool, PAGE_SIZE, DIFF_KV_HEADS, 2 * DIFF_D), 1.0),
        'state_shortconv': nrm((L, DEC_BATCH, SC_WIDTH - 1, SC_CHANNELS), 1.0),
        'state_ffn_conv': nrm((L, DEC_BATCH, FFN_CONV_WIDTH - 1, D_FF), 0.5),
        'page_table': page_table,
        'rel_bias': nrm((REL_BUCKETS, DIFF_HEADS), 0.1),
        'w_ada': nrm((L, D_MODEL, 6 * D_MODEL), 0.5 * D_MODEL ** -0.5),
        'b_ada': nrm((L, 6 * D_MODEL), 0.02),
        'g_norm1': gain((L, D_MODEL)),
        'w_in': nrm((L, D_MODEL, D_IN), D_MODEL ** -0.5),
        'g_cq': gain((L, MLA_Q_RANK)),
        'g_ckv': gain((L, MLA_KV_RANK)),
        'w_uq': nrm((L, MLA_Q_RANK, MLA_HEADS, MLA_D_NOPE + MLA_D_ROPE), MLA_Q_RANK ** -0.5),
        'w_uk': nrm((L, MLA_KV_RANK, MLA_HEADS, MLA_D_NOPE), MLA_KV_RANK ** -0.5),
        'w_uv': nrm((L, MLA_KV_RANK, MLA_HEADS, MLA_D_V), MLA_KV_RANK ** -0.5),
        'w_oa': nrm((L, MLA_HEADS * MLA_D_V, D_MODEL), (MLA_HEADS * MLA_D_V) ** -0.5),
        'w_sc': nrm((L, SC_WIDTH, SC_CHANNELS), SC_WIDTH ** -0.5),
        'w_ob': nrm((L, SC_CHANNELS, D_MODEL), SC_CHANNELS ** -0.5),
        'lam_q1': nrm((L, DIFF_D), 0.1),
        'lam_k1': nrm((L, DIFF_D), 0.1),
        'lam_q2': nrm((L, DIFF_D), 0.1),
        'lam_k2': nrm((L, DIFF_D), 0.1),
        'g_subln': gain((L, 2 * DIFF_D)),
        'w_oc': nrm((L, DIFF_HEADS * 2 * DIFF_D, D_MODEL), (DIFF_HEADS * 2 * DIFF_D) ** -0.5),
        'w_o': nrm((L, D_MODEL, D_MODEL), D_MODEL ** -0.5),
        'g_norm2': gain((L, D_MODEL)),
        'w_gate': nrm((L, D_MODEL, D_FF), D_MODEL ** -0.5),
        'w_up': nrm((L, D_MODEL, D_FF), D_MODEL ** -0.5),
        'w_ffn_conv': nrm((L, FFN_CONV_WIDTH, D_FF), FFN_CONV_WIDTH ** -0.5),
        'w_down': nrm((L, D_FF, D_MODEL), D_FF ** -0.5),
        'g_final': gain((D_MODEL,)),
    }


def reference(x_prompt, x_sample, c_prompt, c_sample, cache_mla_ckv, cache_mla_krope,
              cache_diff_k, cache_diff_v, state_shortconv, state_ffn_conv, page_table,
              rel_bias, w_ada, b_ada, g_norm1, w_in, g_cq, g_ckv, w_uq, w_uk, w_uv, w_oa,
              w_sc, w_ob, lam_q1, lam_k1, lam_q2, lam_k2, g_subln, w_oc, w_o, g_norm2,
              w_gate, w_up, w_ffn_conv, w_down, g_final):
    past_len = page_table.shape[1] * PAGE_SIZE
    bp, dt_p = x_prompt.shape[0], x_prompt.dtype

    def gather_pages(pool):
        g = pool[page_table]
        return g.reshape(g.shape[0], g.shape[1] * g.shape[2], *g.shape[3:])

    xp, xs = x_prompt, x_sample
    p_new = ([], [], [], [], [], [])
    s_new = ([], [], [], [], [], [])
    for l in range(DEPTH):
        lp = {'w_ada': w_ada[l], 'b_ada': b_ada[l], 'g_norm1': g_norm1[l], 'w_in': w_in[l],
              'g_cq': g_cq[l], 'g_ckv': g_ckv[l], 'w_uq': w_uq[l], 'w_uk': w_uk[l],
              'w_uv': w_uv[l], 'w_oa': w_oa[l], 'w_sc': w_sc[l], 'w_ob': w_ob[l],
              'lam_q1': lam_q1[l], 'lam_k1': lam_k1[l], 'lam_q2': lam_q2[l], 'lam_k2': lam_k2[l],
              'g_subln': g_subln[l], 'w_oc': w_oc[l], 'w_o': w_o[l], 'g_norm2': g_norm2[l],
              'w_gate': w_gate[l], 'w_up': w_up[l], 'w_ffn_conv': w_ffn_conv[l], 'w_down': w_down[l]}
        lam_init = 0.8 - 0.6 * math.exp(-0.3 * l)
        xp, st_p = _layer(
            xp, c_prompt, 0,
            jnp.zeros((bp, 0, MLA_KV_RANK), dt_p), jnp.zeros((bp, 0, MLA_D_ROPE), dt_p),
            jnp.zeros((bp, 0, DIFF_KV_HEADS, 2 * DIFF_D), dt_p),
            jnp.zeros((bp, 0, DIFF_KV_HEADS, 2 * DIFF_D), dt_p),
            jnp.zeros((bp, SC_WIDTH - 1, SC_CHANNELS), dt_p),
            jnp.zeros((bp, FFN_CONV_WIDTH - 1, D_FF), dt_p),
            rel_bias, lp, lam_init)
        xs, st_s = _layer(
            xs, c_sample, past_len,
            gather_pages(cache_mla_ckv[l]), gather_pages(cache_mla_krope[l]),
            gather_pages(cache_diff_k[l]), gather_pages(cache_diff_v[l]),
            state_shortconv[l], state_ffn_conv[l],
            rel_bias, lp, lam_init)
        for i in range(6):
            p_new[i].append(st_p[i])
            s_new[i].append(st_s[i])

    y_prompt = rmsnorm(xp, g_final)
    y_sample = rmsnorm(xs, g_final)
    p_ckv, p_krope, p_dk, p_dv, p_sc, p_ffn = [jnp.stack(a, axis=0) for a in p_new]
    s_ckv, s_krope, s_dk, s_dv, s_sc, s_ffn = [jnp.stack(a, axis=0) for a in s_new]
    return (y_prompt, y_sample, p_ckv, p_krope, p_dk, p_dv, p_sc, p_ffn,
            s_ckv, s_krope, s_dk, s_dv, s_sc, s_ffn)
```

```python
import functools
import math

import jax
import jax.numpy as jnp
import numpy as np
from jax import lax
from jax.experimental import pallas as pl
from jax.experimental.pallas import tpu as pltpu

F32 = jnp.float32
BF16 = jnp.bfloat16

EPS = 1e-6
NEG = -1e30
ROPE_THETA = 10000.0
REL_BUCKETS = 32
REL_MAX_DIST = 128
LANES = 128
BAND = 128
VMEM_LIMIT = 56 * 1024 * 1024


def _cparams(sem):
    return pltpu.CompilerParams(dimension_semantics=sem, vmem_limit_bytes=VMEM_LIMIT)


def _const_spec(shape):
    nd = len(shape)
    return pl.BlockSpec(tuple(shape), lambda *_: (0,) * nd, pipeline_mode=pl.Buffered(1))


def _row_spec(arr, tm):
    if arr.shape[1] == 1:
        return pl.BlockSpec((1, 1, arr.shape[2]), lambda b, s: (b, 0, 0))
    return pl.BlockSpec((1, tm, arr.shape[2]), lambda b, s: (b, s, 0))


def _rms(x, g):
    return x * lax.rsqrt(jnp.mean(x * x, axis=-1, keepdims=True) + EPS) * g


def _nt_dot(a, b):
    return lax.dot_general(a, b, (((1,), (1,)), ((), ())), preferred_element_type=F32)


def _dot(a, b):
    return jnp.dot(a, b, preferred_element_type=F32)


def _ada_kernel(c_ref, w_ref, b_ref, o_ref):
    c = c_ref[...]
    a = (c * jax.nn.sigmoid(c)).astype(BF16)
    o_ref[0] = _dot(a, w_ref[0].astype(BF16)) + b_ref[0]


def _adaln(c_all, w_ada, b_ada):
    n_layers, d, n = w_ada.shape
    m = c_all.shape[0]
    tn = 1024 if n % 1024 == 0 else n
    return pl.pallas_call(
        _ada_kernel,
        grid=(n_layers, n // tn),
        in_specs=[pl.BlockSpec((m, d), lambda l, j: (0, 0)),
                  pl.BlockSpec((1, d, tn), lambda l, j: (l, 0, j)),
                  pl.BlockSpec((1, 1, tn), lambda l, j: (l, 0, j))],
        out_specs=pl.BlockSpec((1, m, tn), lambda l, j: (l, 0, j)),
        out_shape=jax.ShapeDtypeStruct((n_layers, m, n), F32),
        compiler_params=_cparams(("parallel", "parallel")),
        name="adaln",
    )(c_all, w_ada, b_ada.reshape(n_layers, 1, n))


def _rope128(x, c, sa, sb):
    return x * c + pltpu.roll(x, LANES - 16, 1) * sa + pltpu.roll(x, 16, 1) * sb


def _in_kernel(n_heads, mla_scale, diff_scale,
               x_ref, sh_ref, sc_ref, gn_ref, cos_ref, sa_ref, sb_ref,
               wq_ref, wkv_ref, wkr_ref, wb_ref, wc_ref, wx_ref, wdq_ref, wdk_ref, wdv_ref,
               wga_ref, wgb_ref, wgc_ref, gcq_ref, gckv_ref, wuqn_ref, wuqr_ref, wukt_ref,
               qcat_o, ckv_o, kr_o, kcat_o, zb_o, v_o, qd1_o, qd2_o, dk_o, dv_o, kd_o, vd_o,
               ga_o, gb_o, gc_o):
    x = x_ref[0]
    h = _rms(x, gn_ref[...]) * (1.0 + sc_ref[0]) + sh_ref[0]
    hb = h.astype(BF16)
    cos, sa, sb = cos_ref[...], sa_ref[...], sb_ref[...]

    cqb = _rms(_dot(hb, wq_ref[...]), gcq_ref[...]).astype(BF16)
    for hd in range(n_heads):
        qn = _dot(cqb, wuqn_ref[hd]).astype(BF16)
        qcat_o[0, hd, :, 0:256] = (_dot(qn, wukt_ref[hd]) * mla_scale).astype(BF16)
        qr = _rope128(_dot(cqb, wuqr_ref[hd]), cos, sa, sb)
        qcat_o[0, hd, :, 256:384] = (qr * mla_scale).astype(BF16)

    ckv = _rms(_dot(hb, wkv_ref[...]), gckv_ref[...])
    ckv_o[0] = ckv
    kr = _rope128(_dot(hb, wkr_ref[...]), cos, sa, sb)
    kr_o[0] = kr[:, 0:32]
    kcat_o[0, :, 0:256] = ckv.astype(BF16)
    kcat_o[0, :, 256:384] = kr.astype(BF16)

    zb_o[0] = _dot(hb, wb_ref[...])
    v_o[0] = _dot(hb, wc_ref[...]) * _dot(hb, wx_ref[...])

    dq = _dot(hb, wdq_ref[...]) * diff_scale
    lane = lax.broadcasted_iota(jnp.int32, (dq.shape[0], 64), 1)
    for hd in range(n_heads):
        sl = dq[:, 64 * hd:64 * hd + 64]
        qd1_o[0, hd] = jnp.where(lane < 32, sl, 0.0).astype(BF16)
        qd2_o[0, hd] = jnp.where(lane >= 32, sl, 0.0).astype(BF16)
    dk = _dot(hb, wdk_ref[...])
    dv = _dot(hb, wdv_ref[...])
    dk_o[0] = dk
    dv_o[0] = dv
    for g in range(dk.shape[1] // 64):
        kd_o[0, g] = dk[:, 64 * g:64 * g + 64].astype(BF16)
        vd_o[0, g] = dv[:, 64 * g:64 * g + 64].astype(BF16)

    ga_o[0] = _dot(hb, wga_ref[...])
    gb_o[0] = _dot(hb, wgb_ref[...])
    gc_o[0] = _dot(hb, wgc_ref[...])


def _in_proj(x, sh, sc, lw, tabs, tm):
    b, s, d = x.shape
    n_heads = lw["wuqn"].shape[0]
    n_groups = lw["wdk"].shape[1] // 64
    tm = min(tm, s)
    consts = [lw[k] for k in ("wq", "wkv", "wkr", "wb", "wc", "wx", "wdq", "wdk", "wdv", "wga", "wgb", "wgc",
                              "gcq", "gckv", "wuqn", "wuqr", "wukt")]
    tok = lambda width: pl.BlockSpec((1, tm, width), lambda bi, si: (bi, si, 0))
    head = lambda nh, width: pl.BlockSpec((1, nh, tm, width), lambda bi, si: (bi, 0, si, 0))
    tab_spec = pl.BlockSpec((tm, LANES), lambda bi, si: (si, 0))
    in_specs = ([tok(d), _row_spec(sh, tm), _row_spec(sc, tm), _const_spec(lw["gn1"].shape),
                 tab_spec, tab_spec, tab_spec] + [_const_spec(c.shape) for c in consts])
    outs = [((b, n_heads, s, 384), BF16, head(n_heads, 384)),
            ((b, s, 256), F32, tok(256)),
            ((b, s, 32), F32, tok(32)),
            ((b, s, 384), BF16, tok(384)),
            ((b, s, 512), F32, tok(512)),
            ((b, s, 512), F32, tok(512)),
            ((b, n_heads, s, 64), BF16, head(n_heads, 64)),
            ((b, n_heads, s, 64), BF16, head(n_heads, 64)),
            ((b, s, 64 * n_groups), F32, tok(64 * n_groups)),
            ((b, s, 64 * n_groups), F32, tok(64 * n_groups)),
            ((b, n_groups, s, 64), BF16, head(n_groups, 64)),
            ((b, n_groups, s, 64), BF16, head(n_groups, 64)),
            ((b, s, d), F32, tok(d)), ((b, s, d), F32, tok(d)), ((b, s, d), F32, tok(d))]
    kern = functools.partial(_in_kernel, n_heads, lw["mla_scale"], lw["diff_scale"])
    return pl.pallas_call(
        kern,
        grid=(b, s // tm),
        in_specs=in_specs,
        out_specs=[o[2] for o in outs],
        out_shape=[jax.ShapeDtypeStruct(o[0], o[1]) for o in outs],
        compiler_params=_cparams(("parallel", "parallel")),
        name="in_proj",
    )(x, sh, sc, lw["gn1"], tabs[0], tabs[1], tabs[2], *consts)


def _causal_pairs(nq, tq, tk):
    qi, ki = [], []
    for i in range(nq):
        for j in range((i * tq + tq - 1) // tk + 1):
            qi.append(i)
            ki.append(j)
    return jnp.asarray(np.array(qi, np.int32)), jnp.asarray(np.array(ki, np.int32))


def _mla_prefill_kernel(n_heads, tq, tk, qi_tab, ki_tab, q_ref, k_ref, wuv_ref, o_ref, m_s, l_s, acc_s):
    step = pl.program_id(1)
    qi, ki = qi_tab[step], ki_tab[step]
    last = (qi * tq + tq - 1) // tk
    rows = n_heads * tq

    @pl.when(ki == 0)
    def _():
        m_s[...] = jnp.full(m_s.shape, NEG, F32)
        l_s[...] = jnp.zeros(l_s.shape, F32)
        acc_s[...] = jnp.zeros(acc_s.shape, F32)

    q = q_ref[0].reshape(rows, 384)
    k = k_ref[0]
    s = _nt_dot(q, k)
    qpos = qi * tq + (lax.broadcasted_iota(jnp.int32, (rows, tk), 0) & (tq - 1))
    kpos = ki * tk + lax.broadcasted_iota(jnp.int32, (rows, tk), 1)
    s = jnp.where(kpos <= qpos, s, NEG)
    m_prev = m_s[...]
    m_new = jnp.maximum(m_prev, jnp.max(s, axis=1, keepdims=True))
    alpha = jnp.exp(m_prev - m_new)
    p = jnp.exp(s - m_new)
    l_s[...] = alpha * l_s[...] + jnp.sum(p, axis=1, keepdims=True)
    acc_s[...] = alpha * acc_s[...] + _dot(p.astype(BF16), k[:, 0:256])
    m_s[...] = m_new

    @pl.when(ki == last)
    def _():
        o_lat = (acc_s[...] / l_s[...]).astype(BF16)
        for hd in range(n_heads):
            o_ref[0, :, 64 * hd:64 * hd + 64] = _dot(o_lat[hd * tq:(hd + 1) * tq], wuv_ref[hd]).astype(o_ref.dtype)


def _mla_prefill(qcat, kcat, wuv, tq, tk):
    b, n_heads, s, _ = qcat.shape
    tq, tk = min(tq, s), min(tk, s)
    assert tq & (tq - 1) == 0 and s % tq == 0 and s % tk == 0
    qi_tab, ki_tab = _causal_pairs(s // tq, tq, tk)
    rows = n_heads * tq
    grid_spec = pltpu.PrefetchScalarGridSpec(
        num_scalar_prefetch=2,
        grid=(b, int(qi_tab.shape[0])),
        in_specs=[pl.BlockSpec((1, n_heads, tq, 384), lambda bi, st, qt, kt: (bi, 0, qt[st], 0)),
                  pl.BlockSpec((1, tk, 384), lambda bi, st, qt, kt: (bi, kt[st], 0)),
                  pl.BlockSpec(wuv.shape, lambda bi, st, qt, kt: (0, 0, 0))],
        out_specs=pl.BlockSpec((1, tq, 64 * n_heads), lambda bi, st, qt, kt: (bi, qt[st], 0)),
        scratch_shapes=[pltpu.VMEM((rows, 1), F32), pltpu.VMEM((rows, 1), F32), pltpu.VMEM((rows, 256), F32)])
    return pl.pallas_call(
        functools.partial(_mla_prefill_kernel, n_heads, tq, tk),
        grid_spec=grid_spec,
        out_shape=jax.ShapeDtypeStruct((b, s, 64 * n_heads), BF16),
        compiler_params=_cparams(("parallel", "arbitrary")),
        name="mla_prefill",
    )(qi_tab, ki_tab, qcat, kcat, wuv)


def _lambda(lam_ref, lam_init):
    a = lam_ref[...]
    t1 = jnp.sum(a[0:1] * a[1:2], axis=1, keepdims=True)
    t2 = jnp.sum(a[2:3] * a[3:4], axis=1, keepdims=True)
    return jnp.exp(t1) - jnp.exp(t2) + lam_init


def _diff_prefill_kernel(rep, td, lam_init, qi_tab, ki_tab, q1_ref, q2_ref, k_ref, v_ref, dt_ref, pt_ref,
                         lam_ref, gs_ref, o_ref, bias_s, m_s, l_s, acc_s):
    step = pl.program_id(2)
    qi, ki = qi_tab[step], ki_tab[step]
    rows = rep * td
    nb = td // BAND

    @pl.when(step == 0)
    def _():
        bias_s[...] = jnp.zeros(bias_s.shape, F32)
        for hd in range(rep):
            for i in range(nb):
                r0 = hd * td + i * BAND
                bias_s[0, r0:r0 + BAND, i * BAND:(i + 1) * BAND] = dt_ref[hd]
                if i > 0:
                    bias_s[0, r0:r0 + BAND, (i - 1) * BAND:i * BAND] = pt_ref[hd]
                if i + 1 < nb:
                    bias_s[0, r0:r0 + BAND, (i + 1) * BAND:td] = jnp.full((BAND, td - (i + 1) * BAND), NEG, F32)
            bias_s[1, hd * td:hd * td + BAND, td - BAND:td] = pt_ref[hd]

    @pl.when(ki == 0)
    def _():
        m_s[...] = jnp.full(m_s.shape, NEG, F32)
        l_s[...] = jnp.zeros(l_s.shape, F32)
        acc_s[...] = jnp.zeros(acc_s.shape, F32)

    def process(bias_idx):
        k = k_ref[0, 0]
        v = v_ref[0, 0]
        for mp, q_ref in enumerate((q1_ref, q2_ref)):
            s = _nt_dot(q_ref[0].reshape(rows, 64), k)
            if bias_idx is not None:
                s = s + bias_s[bias_idx]
            m_prev = m_s[mp]
            m_new = jnp.maximum(m_prev, jnp.max(s, axis=1, keepdims=True))
            alpha = jnp.exp(m_prev - m_new)
            p = jnp.exp(s - m_new)
            l_s[mp] = alpha * l_s[mp] + jnp.sum(p, axis=1, keepdims=True)
            acc_s[mp] = alpha * acc_s[mp] + _dot(p.astype(BF16), v)
            m_s[mp] = m_new

    @pl.when(ki == qi)
    def _():
        process(0)

    @pl.when(ki == qi - 1)
    def _():
        process(1)

    @pl.when(ki < qi - 1)
    def _():
        process(None)

    @pl.when(ki == qi)
    def _():
        lam = _lambda(lam_ref, lam_init)
        o = acc_s[0] / l_s[0] - lam * (acc_s[1] / l_s[1])
        o = _rms(o, gs_ref[...]) * (1.0 - lam_init)
        for hd in range(rep):
            o_ref[0, :, 64 * hd:64 * hd + 64] = o[hd * td:(hd + 1) * td].astype(o_ref.dtype)


def _diff_prefill(qd1, qd2, kd, vd, dtile, ptile, lam_par, gsub, lam_init, td):
    b, n_heads, s, _ = qd1.shape
    n_groups = kd.shape[1]
    rep = n_heads // n_groups
    td = min(td, s)
    assert td % BAND == 0 and s % td == 0
    qi_tab, ki_tab = _causal_pairs(s // td, td, td)
    rows = rep * td
    qspec = pl.BlockSpec((1, rep, td, 64), lambda bi, g, st, qt, kt: (bi, g, qt[st], 0))
    kspec = pl.BlockSpec((1, 1, td, 64), lambda bi, g, st, qt, kt: (bi, g, kt[st], 0))
    tspec = pl.BlockSpec((rep, BAND, BAND), lambda bi, g, st, qt, kt: (g, 0, 0))
    grid_spec = pltpu.PrefetchScalarGridSpec(
        num_scalar_prefetch=2,
        grid=(b, n_groups, int(qi_tab.shape[0])),
        in_specs=[qspec, qspec, kspec, kspec, tspec, tspec,
                  pl.BlockSpec(lam_par.shape, lambda bi, g, st, qt, kt: (0, 0)),
                  pl.BlockSpec(gsub.shape, lambda bi, g, st, qt, kt: (0, 0))],
        out_specs=pl.BlockSpec((1, td, 64 * rep), lambda bi, g, st, qt, kt: (bi, qt[st], g)),
        scratch_shapes=[pltpu.VMEM((2, rows, td), F32), pltpu.VMEM((2, rows, 1), F32),
                        pltpu.VMEM((2, rows, 1), F32), pltpu.VMEM((2, rows, 64), F32)])
    return pl.pallas_call(
        functools.partial(_diff_prefill_kernel, rep, td, lam_init),
        grid_spec=grid_spec,
        out_shape=jax.ShapeDtypeStruct((b, s, 64 * n_heads), BF16),
        compiler_params=_cparams(("parallel", "parallel", "arbitrary")),
        name="diff_prefill",
    )(qi_tab, ki_tab, qd1, qd2, kd, vd, dtile, ptile, lam_par, gsub)


def _page_copies(layer, cp, pt_ref, srcs, bufs, sems, seq, chunk, slot):
    out = []
    for j in range(cp):
        pg = pt_ref[seq, chunk * cp + j]
        for a, (src, buf) in enumerate(zip(srcs, bufs)):
            out.append(pltpu.make_async_copy(src.at[layer, pg], buf.at[slot, j], sems.at[a, slot]))
    return out


def _chunk_pipeline(layer, cp, n_chunks, pt_ref, srcs, bufs, sems):
    seq, chunk = pl.program_id(0), pl.program_id(1)
    gstep = seq * n_chunks + chunk
    total = pl.num_programs(0) * n_chunks
    slot = gstep % 2

    @pl.when(gstep == 0)
    def _():
        for c in _page_copies(layer, cp, pt_ref, srcs, bufs, sems, 0, 0, 0):
            c.start()

    @pl.when(gstep + 1 < total)
    def _():
        nxt = gstep + 1
        for c in _page_copies(layer, cp, pt_ref, srcs, bufs, sems, nxt // n_chunks, nxt % n_chunks, 1 - slot):
            c.start()

    for c in _page_copies(layer, cp, pt_ref, srcs, bufs, sems, seq, chunk, slot):
        c.wait()
    return slot


def _online_update(s, v, m_s, l_s, acc_s):
    m_prev = m_s[...]
    m_new = jnp.maximum(m_prev, jnp.max(s, axis=1, keepdims=True))
    alpha = jnp.exp(m_prev - m_new)
    p = jnp.exp(s - m_new)
    l_s[...] = alpha * l_s[...] + jnp.sum(p, axis=1, keepdims=True)
    acc_s[...] = alpha * acc_s[...] + _dot(p.astype(BF16), v)
    m_s[...] = m_new


def _mla_decode_kernel(layer, cp, n_chunks, n_heads, n_new, pt_ref, q_ref, knew_ref, wuv_ref, ckv_hbm, kr_hbm,
                       o_ref, ckv_buf, kr_buf, sems, m_s, l_s, acc_s):
    chunk = pl.program_id(1)
    slot = _chunk_pipeline(layer, cp, n_chunks, pt_ref, (ckv_hbm, kr_hbm), (ckv_buf, kr_buf), sems)
    rows = q_ref.shape[1]

    @pl.when(chunk == 0)
    def _():
        m_s[...] = jnp.full(m_s.shape, NEG, F32)
        l_s[...] = jnp.zeros(l_s.shape, F32)
        acc_s[...] = jnp.zeros(acc_s.shape, F32)

    q = q_ref[0]
    page = ckv_buf.shape[2]
    kc = ckv_buf[slot].reshape(cp * page, 256).astype(BF16)
    kr = kr_buf[slot].reshape(cp * page, 32).astype(BF16)
    s = _nt_dot(q[:, 0:256], kc) + _nt_dot(q[:, 256:288], kr)
    _online_update(s, kc, m_s, l_s, acc_s)

    @pl.when(chunk == n_chunks - 1)
    def _():
        kn = knew_ref[0]
        sn = _nt_dot(q, kn)
        qidx = lax.broadcasted_iota(jnp.int32, sn.shape, 0) // n_heads
        kidx = lax.broadcasted_iota(jnp.int32, sn.shape, 1)
        sn = jnp.where((kidx <= qidx) & (kidx < n_new), sn, NEG)
        _online_update(sn, kn[:, 0:256], m_s, l_s, acc_s)
        o_lat = (acc_s[...] / l_s[...]).astype(BF16)
        full = _dot(o_lat, wuv_ref[...])
        hsel = (lax.broadcasted_iota(jnp.int32, (n_heads, full.shape[1]), 1) // 64
                == lax.broadcasted_iota(jnp.int32, (n_heads, full.shape[1]), 0))
        out_rows = [jnp.sum(jnp.where(hsel, full[i * n_heads:(i + 1) * n_heads], 0.0), axis=0, keepdims=True)
                    for i in range(rows // n_heads)]
        o_ref[0] = jnp.concatenate(out_rows, axis=0)


def _mla_decode(layer, page_table, qdec, knew, wuv_flat, cache_ckv, cache_kr, n_heads, n_new, cp):
    bd, rows, _ = qdec.shape
    n_pages = page_table.shape[1]
    page = cache_ckv.shape[2]
    cp = min(cp, n_pages)
    assert n_pages % cp == 0
    n_chunks = n_pages // cp
    grid_spec = pltpu.PrefetchScalarGridSpec(
        num_scalar_prefetch=1,
        grid=(bd, n_chunks),
        in_specs=[pl.BlockSpec((1, rows, 384), lambda b, c, pt: (b, 0, 0)),
                  pl.BlockSpec((1, 8, 384), lambda b, c, pt: (b, 0, 0)),
                  pl.BlockSpec(wuv_flat.shape, lambda b, c, pt: (0, 0)),
                  pl.BlockSpec(memory_space=pl.ANY),
                  pl.BlockSpec(memory_space=pl.ANY)],
        out_specs=pl.BlockSpec((1, rows // n_heads, 64 * n_heads), lambda b, c, pt: (b, 0, 0)),
        scratch_shapes=[pltpu.VMEM((2, cp, page, 256), F32), pltpu.VMEM((2, cp, page, 32), F32),
                        pltpu.SemaphoreType.DMA((2, 2)),
                        pltpu.VMEM((rows, 1), F32), pltpu.VMEM((rows, 1), F32), pltpu.VMEM((rows, 256), F32)])
    return pl.pallas_call(
        functools.partial(_mla_decode_kernel, layer, cp, n_chunks, n_heads, n_new),
        grid_spec=grid_spec,
        out_shape=jax.ShapeDtypeStruct((bd, rows // n_heads, 64 * n_heads), F32),
        compiler_params=_cparams(("arbitrary", "arbitrary")),
        name="mla_decode",
    )(page_table, qdec, knew, wuv_flat, cache_ckv, cache_kr)


def _diff_decode_kernel(layer, cp, n_chunks, n_heads, rep, n_new, lam_init, pt_ref, q_ref, knew_ref, vnew_ref,
                        bpast_ref, bnew_ref, lam_ref, gs_ref, k_hbm, v_hbm, o_ref, k_buf, v_buf, sems,
                        m_s, l_s, acc_s):
    chunk = pl.program_id(1)
    slot = _chunk_pipeline(layer, cp, n_chunks, pt_ref, (k_hbm, v_hbm), (k_buf, v_buf), sems)
    rows = q_ref.shape[1]
    half = rows // 2

    @pl.when(chunk == 0)
    def _():
        m_s[...] = jnp.full(m_s.shape, NEG, F32)
        l_s[...] = jnp.zeros(l_s.shape, F32)
        acc_s[...] = jnp.zeros(acc_s.shape, F32)

    q = q_ref[0]
    page = k_buf.shape[2]
    width = k_buf.shape[3]
    kc = k_buf[slot].reshape(cp * page, width).astype(BF16)
    vc = v_buf[slot].reshape(cp * page, width).astype(BF16)

    @pl.when(chunk < n_chunks - 1)
    def _():
        _online_update(_nt_dot(q, kc), vc, m_s, l_s, acc_s)

    @pl.when(chunk == n_chunks - 1)
    def _():
        s = _nt_dot(q, kc)
        n_far = cp * page - BAND
        if n_far > 0:
            _online_update(s[:, 0:n_far], vc[0:n_far], m_s, l_s, acc_s)
        _online_update(s[:, n_far:] + bpast_ref[...], vc[n_far:], m_s, l_s, acc_s)
        _online_update(_nt_dot(q, knew_ref[0]) + bnew_ref[...], vnew_ref[0], m_s, l_s, acc_s)
        lam = _lambda(lam_ref, lam_init)
        o = acc_s[...] / l_s[...]
        o = o[0:half] - lam * o[half:rows]
        hidx = lax.broadcasted_iota(jnp.int32, (half, 64), 0) % n_heads
        osel = o[:, 0:64]
        for g in range(1, n_heads // rep):
            osel = jnp.where(hidx // rep == g, o[:, 64 * g:64 * g + 64], osel)
        o_ref[0] = (_rms(osel, gs_ref[...]) * (1.0 - lam_init)).astype(o_ref.dtype)


def _diff_decode(layer, page_table, qdec, knew, vnew, bpast, bnew, lam_par, gsub, cache_k, cache_v,
                 n_heads, rep, n_new, lam_init, cp):
    bd, rows, width = qdec.shape
    n_pages = page_table.shape[1]
    page = cache_k.shape[2]
    cp = min(cp, n_pages)
    assert n_pages % cp == 0 and page % BAND == 0
    n_chunks = n_pages // cp
    full2 = lambda a: pl.BlockSpec(a.shape, lambda b, c, pt: (0, 0))
    grid_spec = pltpu.PrefetchScalarGridSpec(
        num_scalar_prefetch=1,
        grid=(bd, n_chunks),
        in_specs=[pl.BlockSpec((1, rows, width), lambda b, c, pt: (b, 0, 0)),
                  pl.BlockSpec((1, 8, width), lambda b, c, pt: (b, 0, 0)),
                  pl.BlockSpec((1, 8, width), lambda b, c, pt: (b, 0, 0)),
                  full2(bpast), full2(bnew), full2(lam_par), full2(gsub),
                  pl.BlockSpec(memory_space=pl.ANY),
                  pl.BlockSpec(memory_space=pl.ANY)],
        out_specs=pl.BlockSpec((1, rows // 2, 64), lambda b, c, pt: (b, 0, 0)),
        scratch_shapes=[pltpu.VMEM((2, cp, page, width), F32), pltpu.VMEM((2, cp, page, width), F32),
                        pltpu.SemaphoreType.DMA((2, 2)),
                        pltpu.VMEM((rows, 1), F32), pltpu.VMEM((rows, 1), F32), pltpu.VMEM((rows, width), F32)])
    return pl.pallas_call(
        functools.partial(_diff_decode_kernel, layer, cp, n_chunks, n_heads, rep, n_new, lam_init),
        grid_spec=grid_spec,
        out_shape=jax.ShapeDtypeStruct((bd, rows // 2, 64), BF16),
        compiler_params=_cparams(("arbitrary", "arbitrary")),
        name="diff_decode",
    )(page_table, qdec, knew, vnew, bpast, bnew, lam_par, gsub, cache_k, cache_v)


def _merge_kernel(x_ref, g1_ref, oa_ref, ub_ref, oc_ref, ga_ref, gb_ref, gc_ref,
                  woa_ref, wob_ref, woc_ref, wo_ref, o_ref):
    merged = (jax.nn.sigmoid(ga_ref[0]) * _dot(oa_ref[0], woa_ref[...])
              + jax.nn.sigmoid(gb_ref[0]) * _dot(ub_ref[0], wob_ref[...])
              + jax.nn.sigmoid(gc_ref[0]) * _dot(oc_ref[0], woc_ref[...]))
    o_ref[0] = x_ref[0] + g1_ref[0] * _dot(merged.astype(BF16), wo_ref[...])


def _merge(x, g1, oa, ub, oc, ga, gb, gc, lw, tm):
    b, s, d = x.shape
    tm = min(tm, s)
    tok = lambda width: pl.BlockSpec((1, tm, width), lambda bi, si: (bi, si, 0))
    consts = [lw[k] for k in ("woa", "wob", "woc", "wo")]
    return pl.pallas_call(
        _merge_kernel,
        grid=(b, s // tm),
        in_specs=[tok(d), _row_spec(g1, tm), tok(oa.shape[2]), tok(ub.shape[2]), tok(oc.shape[2]),
                  tok(d), tok(d), tok(d)] + [_const_spec(c.shape) for c in consts],
        out_specs=tok(d),
        out_shape=jax.ShapeDtypeStruct((b, s, d), F32),
        compiler_params=_cparams(("parallel", "parallel")),
        name="merge",
    )(x, g1, oa, ub, oc, ga, gb, gc, *consts)


def _ffn_kernel(seq_len, n_chunks, has_final, x_ref, halo_ref, sh_ref, sc_ref, g2_ref, gn_ref, p1_ref, p2_ref,
                wg_ref, wu_ref, wcv_ref, wd_ref, gf_ref, o_ref, u_ref):
    tm = x_ref.shape[1]
    ff = wg_ref.shape[1]
    tf = ff // n_chunks
    tu = u_ref.shape[1]
    x = x_ref[0]
    gn = gn_ref[...]
    h2 = (_rms(x, gn) * (1.0 + sc_ref[0]) + sh_ref[0]).astype(BF16)
    row = lax.broadcasted_iota(jnp.int32, (tm, 1), 0)
    if seq_len is None:
        sc_h = sc_ref[0][0:1] if sc_ref.shape[1] == 1 else sc_ref[0][0:8]
        sh_h = sh_ref[0][0:1] if sh_ref.shape[1] == 1 else sh_ref[0][0:8]
        live = (pl.program_id(1) > 0).astype(F32)
        hh = ((_rms(halo_ref[0], gn) * (1.0 + sc_h) + sh_h) * live).astype(BF16)
        pos = row
    else:
        pos = row % seq_len
    acc = jnp.zeros((tm, x.shape[1]), F32)
    for j in range(n_chunks):
        cols = slice(j * tf, (j + 1) * tf)
        u = _dot(h2, wg_ref[:, cols])
        if seq_len is None:
            uh = _dot(hh, wg_ref[:, cols])
            prev1 = uh[7:8]
            prev2 = jnp.where(row == 0, uh[6:7], uh[7:8])
        else:
            prev1 = p1_ref[0, :, cols]
            prev2 = p2_ref[0, :, cols]
        u1 = jnp.where(pos >= 1, pltpu.roll(u, 1, 0), prev1)
        u2 = jnp.where(pos >= 2, pltpu.roll(u, 2, 0), prev2)
        wcv = wcv_ref[:, cols]
        act = u2 * wcv[0:1] + u1 * wcv[1:2] + u * wcv[2:3]
        gated = (act * jax.nn.sigmoid(act)) * _dot(h2, wu_ref[:, cols])
        acc = acc + _dot(gated.astype(BF16), wd_ref[cols, :])
        u_ref[0, :, cols] = u[tm - tu:tm]
    y = x + g2_ref[0] * acc
    if has_final:
        y = _rms(y, gf_ref[...])
    o_ref[0] = y


def _ffn(x, sh, sc, g2, lw, prev, seq_len, g_final, tm):
    b, s, d = x.shape
    ff = lw["wg"].shape[1]
    tm = min(tm, s)
    n_chunks = 2 if (ff // 2) % LANES == 0 else 1
    tok = lambda width: pl.BlockSpec((1, tm, width), lambda bi, si: (bi, si, 0))
    if prev is None:
        tu = 8
        halo_spec = pl.BlockSpec((1, 8, d), lambda bi, si: (bi, jnp.maximum(si * (tm // 8) - 1, 0), 0))
        p1 = p2 = jnp.zeros((1, 8, LANES), F32)
        pspec = pl.BlockSpec((1, 8, LANES), lambda bi, si: (0, 0, 0))
    else:
        tu = tm
        halo_spec = pl.BlockSpec((1, 8, d), lambda bi, si: (bi, 0, 0))
        p1, p2 = prev
        pspec = tok(ff)
    has_final = g_final is not None
    gf = g_final.reshape(1, -1).astype(F32) if has_final else lw["gn2"]
    consts = [lw["wg"], lw["wu"], lw["wcv"], lw["wd"], gf]
    y, u = pl.pallas_call(
        functools.partial(_ffn_kernel, seq_len, n_chunks, has_final),
        grid=(b, s // tm),
        in_specs=[tok(d), halo_spec, _row_spec(sh, tm), _row_spec(sc, tm), _row_spec(g2, tm),
                  _const_spec(lw["gn2"].shape), pspec, pspec] + [_const_spec(c.shape) for c in consts],
        out_specs=[tok(d), pl.BlockSpec((1, tu, ff), lambda bi, si: (bi, si, 0))],
        out_shape=[jax.ShapeDtypeStruct((b, s, d), F32), jax.ShapeDtypeStruct((b, (s // tm) * tu, ff), F32)],
        compiler_params=_cparams(("parallel", "parallel")),
        name="ffn",
    )(x, x, sh, sc, g2, lw["gn2"], p1, p2, *consts)
    return y, u


def _rope_tables(pos, d_rope):
    half = d_rope // 2
    inv = ROPE_THETA ** (-jnp.arange(half, dtype=F32) / half)
    ang = pos.astype(F32)[:, None] * inv[None, :]
    cos, sin = jnp.cos(ang), jnp.sin(ang)
    pad = ((0, 0), (0, LANES - d_rope))
    c = jnp.pad(jnp.concatenate([cos, cos], axis=1), pad)
    sa = jnp.pad(jnp.concatenate([-sin, jnp.zeros_like(sin)], axis=1), pad)
    sb = jnp.pad(jnp.concatenate([jnp.zeros_like(sin), sin], axis=1), pad)
    return c, sa, sb


def _band_bias(rel_bias):
    n = jnp.arange(BAND, dtype=jnp.int32)
    max_exact = REL_BUCKETS // 2
    large = max_exact + (jnp.log(jnp.maximum(n, 1).astype(F32) / max_exact)
                         / math.log(REL_MAX_DIST / max_exact) * (REL_BUCKETS - max_exact)).astype(jnp.int32)
    bucket = jnp.where(n < max_exact, n, jnp.minimum(large, REL_BUCKETS - 1))
    tbl = rel_bias.astype(F32)
    return (tbl[bucket] - tbl[REL_BUCKETS - 1][None, :]).T


def _band_lookup(band, dist):
    val = band[:, jnp.clip(dist, 0, BAND - 1)]
    val = jnp.where(dist[None] >= BAND, 0.0, val)
    return jnp.where(dist[None] < 0, NEG, val)


def _layer_weights(l, p, sizes):
    offs = np.concatenate([[0], np.cumsum(sizes)])
    w = p["w_in"][l]
    seg = [w[:, offs[i]:offs[i + 1]].astype(BF16) for i in range(12)]
    d_rope = sizes[2]
    w_uq = p["w_uq"][l]
    d_nope = w_uq.shape[2] - d_rope
    row = lambda a: a.reshape(1, -1).astype(F32)
    lw = {
        "wq": seg[0], "wkv": seg[1], "wkr": jnp.pad(seg[2], ((0, 0), (0, LANES - d_rope))),
        "wb": seg[3], "wc": seg[4], "wx": seg[5], "wdq": seg[6], "wdk": seg[7], "wdv": seg[8],
        "wga": seg[9], "wgb": seg[10], "wgc": seg[11],
        "gn1": row(p["g_norm1"][l]), "gcq": row(p["g_cq"][l]), "gckv": row(p["g_ckv"][l]),
        "wuqn": jnp.transpose(w_uq[:, :, :d_nope], (1, 0, 2)).astype(BF16),
        "wuqr": jnp.pad(jnp.transpose(w_uq[:, :, d_nope:], (1, 0, 2)),
                        ((0, 0), (0, 0), (0, LANES - d_rope))).astype(BF16),
        "wukt": jnp.transpose(p["w_uk"][l], (1, 2, 0)).astype(BF16),
        "wuv": jnp.transpose(p["w_uv"][l], (1, 0, 2)).astype(BF16),
        "wuv_flat": p["w_uv"][l].reshape(p["w_uv"].shape[1], -1).astype(BF16),
        "woa": p["w_oa"][l].astype(BF16), "wob": p["w_ob"][l].astype(BF16), "woc": p["w_oc"][l].astype(BF16),
        "wo": p["w_o"][l].astype(BF16),
        "wsc": p["w_sc"][l].astype(F32),
        "lam": jnp.stack([p["lam_q1"][l], p["lam_k1"][l], p["lam_q2"][l], p["lam_k2"][l]]).astype(F32),
        "gsub": row(p["g_subln"][l]),
        "gn2": row(p["g_norm2"][l]),
        "wg": p["w_gate"][l].astype(BF16), "wu": p["w_up"][l].astype(BF16),
        "wcv": p["w_ffn_conv"][l].astype(F32), "wd": p["w_down"][l].astype(BF16),
        "mla_scale": float((d_nope + d_rope) ** -0.5),
        "diff_scale": float((sizes[6] // w_uq.shape[1] // 2) ** -0.5),
    }
    return lw


def _short_conv(v, prev, w):
    ext = jnp.concatenate([prev.astype(v.dtype), v], axis=1)
    s = v.shape[1]
    y = ext[:, 0:s] * w[0] + ext[:, 1:s + 1] * w[1] + ext[:, 2:s + 2] * w[2]
    return y, ext[:, s:s + 2]


def kernel(x_prompt, x_sample, c_prompt, c_sample, cache_mla_ckv, cache_mla_krope, cache_diff_k, cache_diff_v, state_shortconv, state_ffn_conv, page_table, rel_bias, w_ada, b_ada, g_norm1, w_in, g_cq, g_ckv, w_uq, w_uk, w_uv, w_oa, w_sc, w_ob, lam_q1, lam_k1, lam_q2, lam_k2, g_subln, w_oc, w_o, g_norm2, w_gate, w_up, w_ffn_conv, w_down, g_final):
    p = dict(w_in=w_in, g_norm1=g_norm1, g_cq=g_cq, g_ckv=g_ckv, w_uq=w_uq, w_uk=w_uk, w_uv=w_uv, w_oa=w_oa,
             w_sc=w_sc, w_ob=w_ob, lam_q1=lam_q1, lam_k1=lam_k1, lam_q2=lam_q2, lam_k2=lam_k2, g_subln=g_subln,
             w_oc=w_oc, w_o=w_o, g_norm2=g_norm2, w_gate=w_gate, w_up=w_up, w_ffn_conv=w_ffn_conv, w_down=w_down)
    n_layers = w_in.shape[0]
    bp, sp, d = x_prompt.shape
    bs, ss, _ = x_sample.shape
    n_heads = w_uq.shape[2]
    d_rope = cache_mla_krope.shape[-1]
    n_groups = cache_diff_k.shape[3]
    rep = n_heads // n_groups
    page = cache_mla_ckv.shape[2]
    past_len = page_table.shape[1] * page
    sizes = (w_uq.shape[1], cache_mla_ckv.shape[-1], d_rope, w_sc.shape[2], w_sc.shape[2], w_sc.shape[2],
             w_oc.shape[1], n_groups * cache_diff_k.shape[-1], n_groups * cache_diff_v.shape[-1], d, d, d)
    assert sum(sizes) == w_in.shape[2]
    page_table = page_table.astype(jnp.int32)

    n_c = bp + bs
    c_all = jnp.pad(jnp.concatenate([c_prompt, c_sample], axis=0), ((0, (-n_c) % 8), (0, 0)))
    mod = _adaln(c_all, w_ada, b_ada)

    tabs_p = _rope_tables(jnp.arange(sp, dtype=jnp.int32), d_rope)
    tabs_s = _rope_tables(past_len + (jnp.arange(bs * ss, dtype=jnp.int32) % ss), d_rope)
    band = _band_bias(rel_bias)
    ar = jnp.arange(BAND, dtype=jnp.int32)
    dtile = _band_lookup(band, ar[:, None] - ar[None, :])
    ptile = _band_lookup(band, BAND + ar[:, None] - ar[None, :])
    qi_rows = jnp.repeat(jnp.arange(ss, dtype=jnp.int32), n_heads)
    bpast = _band_lookup(band, BAND + qi_rows[:, None] - ar[None, :])
    new_idx = jnp.arange(8, dtype=jnp.int32)
    dist_new = jnp.where(new_idx[None, :] < ss, qi_rows[:, None] - new_idx[None, :], -1)
    bnew = _band_lookup(band, dist_new)
    hrow = jnp.tile(jnp.arange(n_heads), ss)
    pick = lambda a: jnp.tile(a[hrow, jnp.arange(ss * n_heads)], (2, 1))
    bpast, bnew = pick(bpast), pick(bnew)

    xp, xs = x_prompt, x_sample.reshape(1, bs * ss, d)
    p_new, s_new = [], []
    for l in range(n_layers):
        lw = _layer_weights(l, p, sizes)
        lam_init = 0.8 - 0.6 * math.exp(-0.3 * l)
        mods_p = [m[:, None, :] for m in jnp.split(mod[l, :bp], 6, axis=-1)]
        mods_s = [jnp.repeat(m, ss, axis=0)[None] for m in jnp.split(mod[l, bp:n_c], 6, axis=-1)]

        sh1, sc1, g1, sh2, sc2, g2 = mods_p
        (qcat, ckv, kr, kcat, zb, v, qd1, qd2, dk, dv, kd, vd, ga, gb, gc) = _in_proj(xp, sh1, sc1, lw, tabs_p, 256)
        oa = _mla_prefill(qcat, kcat, lw["wuv"], 128, 512)
        oc = _diff_prefill(qd1, qd2, kd, vd, dtile, ptile, lw["lam"], lw["gsub"], lam_init, 512)
        y_sc, sc_state = _short_conv(v, jnp.zeros((bp, 2, v.shape[2]), F32), lw["wsc"])
        xp = _merge(xp, g1, oa, (zb * y_sc).astype(BF16), oc, ga, gb, gc, lw, 256)
        xp, u_tail = _ffn(xp, sh2, sc2, g2, lw, None, None, g_final if l == n_layers - 1 else None, 256)
        p_new.append((ckv, kr, dk.reshape(bp, sp, n_groups, -1), dv.reshape(bp, sp, n_groups, -1),
                      sc_state, u_tail[:, -2:]))

        sh1, sc1, g1, sh2, sc2, g2 = mods_s
        (qcat, ckv, kr, kcat, zb, v, qd1, qd2, dk, dv, kd, vd, ga, gb, gc) = _in_proj(xs, sh1, sc1, lw, tabs_s, 256)
        pad_new = lambda a: jnp.pad(a.reshape(bs, ss, a.shape[-1]), ((0, 0), (0, 8 - ss), (0, 0)))
        qdec = jnp.transpose(qcat[0].reshape(n_heads, bs, ss, 384), (1, 2, 0, 3)).reshape(bs, ss * n_heads, 384)
        oa = _mla_decode(l, page_table, qdec, pad_new(kcat[0]), lw["wuv_flat"], cache_mla_ckv, cache_mla_krope,
                         n_heads, ss, 16)
        def dec_rows(qd):
            a = jnp.transpose(qd[0].reshape(n_groups, rep, bs, ss, 64), (2, 3, 0, 1, 4))
            a = a[:, :, :, :, None, :] * jnp.eye(n_groups, dtype=a.dtype)[None, None, :, None, :, None]
            return a.reshape(bs, ss * n_heads, n_groups * 64)
        qd_dec = jnp.concatenate([dec_rows(qd1), dec_rows(qd2)], axis=1)
        ck = cache_diff_k.reshape(*cache_diff_k.shape[:3], -1)
        cv = cache_diff_v.reshape(*cache_diff_v.shape[:3], -1)
        oc = _diff_decode(l, page_table, qd_dec, pad_new(dk[0].astype(BF16)), pad_new(dv[0].astype(BF16)),
                          bpast, bnew, lw["lam"], lw["gsub"], ck, cv, n_heads, rep, ss, lam_init, 16)
        oa = oa.reshape(1, bs * ss, -1).astype(BF16)
        oc = oc.reshape(1, bs * ss, -1)
        y_sc, sc_state = _short_conv(v.reshape(bs, ss, -1), state_shortconv[l], lw["wsc"])
        ub = (zb * y_sc.reshape(1, bs * ss, -1)).astype(BF16)
        xs = _merge(xs, g1, oa, ub, oc, ga, gb, gc, lw, 256)
        st = state_ffn_conv[l].astype(F32)
        zero = jnp.zeros((bs, ss, st.shape[2]), F32)
        p1 = zero.at[:, 0].set(st[:, 1]).reshape(1, bs * ss, -1)
        p2 = zero.at[:, 0].set(st[:, 0]).at[:, 1].set(st[:, 1]).reshape(1, bs * ss, -1)
        xs, u_all = _ffn(xs, sh2, sc2, g2, lw, (p1, p2), ss, g_final if l == n_layers - 1 else None, 256)
        s_new.append((ckv.reshape(bs, ss, -1), kr.reshape(bs, ss, -1), dk.reshape(bs, ss, n_groups, -1),
                      dv.reshape(bs, ss, n_groups, -1), sc_state, u_all.reshape(bs, ss, -1)[:, ss - 2:]))

    stack = lambda lst, i: jnp.stack([e[i] for e in lst], axis=0)
    return ((xp, xs.reshape(bs, ss, d)) + tuple(stack(p_new, i) for i in range(6))
            + tuple(stack(s_new, i) for i in range(6)))
```

```python
import functools
import math

import jax
import jax.numpy as jnp
import numpy as np
from jax import lax
from jax.experimental import pallas as pl
from jax.experimental.pallas import tpu as pltpu

F32 = jnp.float32
BF16 = jnp.bfloat16

EPS = 1e-6
NEG = -1e30
ROPE_THETA = 10000.0
REL_BUCKETS = 32
REL_MAX_DIST = 128
LANES = 128
BAND = 128
VMEM_LIMIT = 56 * 1024 * 1024
LOG2E = 1.4426950408889634

TOKEN_TILE = 256
MLA_TQ, MLA_TK = 256, 512
DIFF_TILE = 512
DECODE_PAGES = 32
SCORE_AHEAD = 3


def _cparams(sem):
    return pltpu.CompilerParams(dimension_semantics=sem, vmem_limit_bytes=VMEM_LIMIT)


def _const_spec(shape):
    nd = len(shape)
    return pl.BlockSpec(tuple(shape), lambda *_: (0,) * nd, pipeline_mode=pl.Buffered(1))


def _row_spec(arr, tm):
    if arr.shape[1] == 1:
        return pl.BlockSpec((1, 1, arr.shape[2]), lambda b, s: (b, 0, 0))
    return pl.BlockSpec((1, tm, arr.shape[2]), lambda b, s: (b, s, 0))


def _rms(x, g):
    return x * lax.rsqrt(jnp.mean(x * x, axis=-1, keepdims=True) + EPS) * g


def _nt_dot(a, b):
    return lax.dot_general(a, b, (((1,), (1,)), ((), ())), preferred_element_type=F32)


def _dot(a, b):
    return jnp.dot(a, b, preferred_element_type=F32)


def _ada_kernel(c_ref, w_ref, b_ref, o_ref):
    c = c_ref[...]
    a = (c * jax.nn.sigmoid(c)).astype(BF16)
    o_ref[0] = _dot(a, w_ref[0].astype(BF16)) + b_ref[0]


def _adaln(c_all, w_ada, b_ada):
    n_layers, d, n = w_ada.shape
    m = c_all.shape[0]
    tn = 1024 if n % 1024 == 0 else n
    return pl.pallas_call(
        _ada_kernel,
        grid=(n_layers, n // tn),
        in_specs=[pl.BlockSpec((m, d), lambda l, j: (0, 0)),
                  pl.BlockSpec((1, d, tn), lambda l, j: (l, 0, j)),
                  pl.BlockSpec((1, 1, tn), lambda l, j: (l, 0, j))],
        out_specs=pl.BlockSpec((1, m, tn), lambda l, j: (l, 0, j)),
        out_shape=jax.ShapeDtypeStruct((n_layers, m, n), F32),
        compiler_params=_cparams(("parallel", "parallel")),
        name="adaln",
    )(c_all, w_ada, b_ada.reshape(n_layers, 1, n))


def _rope128(x, c, sa, sb):
    return x * c + pltpu.roll(x, LANES - 16, 1) * sa + pltpu.roll(x, 16, 1) * sb


def _in_kernel(n_heads, mla_scale, diff_scale,
               x_ref, sh_ref, sc_ref, gn_ref, cos_ref, sa_ref, sb_ref,
               wq_ref, wkv_ref, wkr_ref, wb_ref, wc_ref, wx_ref, wdq_ref, wdk_ref, wdv_ref,
               wga_ref, wgb_ref, wgc_ref, gcq_ref, gckv_ref, wuqn_ref, wuqr_ref, wukt_ref,
               qcat_o, ckv_o, kr_o, kcat_o, ckvt_o, zb_o, v_o, qd1_o, qd2_o, dk_o, dv_o, kd_o, vdt_o,
               ga_o, gb_o, gc_o):
    x = x_ref[0]
    h = _rms(x, gn_ref[...]) * (1.0 + sc_ref[0]) + sh_ref[0]
    hb = h.astype(BF16)
    cos, sa, sb = cos_ref[...], sa_ref[...], sb_ref[...]

    cqb = _rms(_dot(hb, wq_ref[...]), gcq_ref[...]).astype(BF16)
    for hd in range(n_heads):
        qn = _dot(cqb, wuqn_ref[hd]).astype(BF16)
        qcat_o[0, hd, :, 0:256] = (_dot(qn, wukt_ref[hd]) * mla_scale).astype(BF16)
        qr = _rope128(_dot(cqb, wuqr_ref[hd]), cos, sa, sb)
        qcat_o[0, hd, :, 256:384] = (qr * mla_scale).astype(BF16)

    ckv = _rms(_dot(hb, wkv_ref[...]), gckv_ref[...])
    ckv_o[0] = ckv
    kr = _rope128(_dot(hb, wkr_ref[...]), cos, sa, sb)
    kr_o[0] = kr[:, 0:32]
    kcat_o[0, :, 0:256] = ckv.astype(BF16)
    kcat_o[0, :, 256:384] = kr.astype(BF16)
    ckvt_o[0] = ckv.T.astype(BF16)

    zb_o[0] = _dot(hb, wb_ref[...])
    v_o[0] = _dot(hb, wc_ref[...]) * _dot(hb, wx_ref[...])

    dq = _dot(hb, wdq_ref[...]) * diff_scale
    lane = lax.broadcasted_iota(jnp.int32, (dq.shape[0], 64), 1)
    for hd in range(n_heads):
        sl = dq[:, 64 * hd:64 * hd + 64]
        qd1_o[0, hd] = jnp.where(lane < 32, sl, 0.0).astype(BF16)
        qd2_o[0, hd] = jnp.where(lane >= 32, sl, 0.0).astype(BF16)
    dk = _dot(hb, wdk_ref[...])
    dv = _dot(hb, wdv_ref[...])
    dk_o[0] = dk
    dv_o[0] = dv
    dvt = dv.T
    for g in range(dk.shape[1] // 64):
        kd_o[0, g] = dk[:, 64 * g:64 * g + 64].astype(BF16)
        vdt_o[0, g] = dvt[64 * g:64 * g + 64].astype(BF16)

    ga_o[0] = _dot(hb, wga_ref[...])
    gb_o[0] = _dot(hb, wgb_ref[...])
    gc_o[0] = _dot(hb, wgc_ref[...])


def _in_proj(x, sh, sc, lw, tabs):
    b, s, d = x.shape
    n_heads = lw["wuqn"].shape[0]
    n_groups = lw["wdk"].shape[1] // 64
    tm = min(TOKEN_TILE, s)
    consts = [lw[k] for k in ("wq", "wkv", "wkr", "wb", "wc", "wx", "wdq", "wdk", "wdv", "wga", "wgb", "wgc",
                              "gcq", "gckv", "wuqn", "wuqr", "wukt")]
    tok = lambda width: pl.BlockSpec((1, tm, width), lambda bi, si: (bi, si, 0))
    head = lambda nh, width: pl.BlockSpec((1, nh, tm, width), lambda bi, si: (bi, 0, si, 0))
    tab_spec = pl.BlockSpec((tm, LANES), lambda bi, si: (si, 0))
    in_specs = ([tok(d), _row_spec(sh, tm), _row_spec(sc, tm), _const_spec(lw["gn1"].shape),
                 tab_spec, tab_spec, tab_spec] + [_const_spec(c.shape) for c in consts])
    outs = [((b, n_heads, s, 384), BF16, head(n_heads, 384)),
            ((b, s, 256), F32, tok(256)),
            ((b, s, 32), F32, tok(32)),
            ((b, s, 384), BF16, tok(384)),
            ((b, 256, s), BF16, pl.BlockSpec((1, 256, tm), lambda bi, si: (bi, 0, si))),
            ((b, s, 512), F32, tok(512)),
            ((b, s, 512), F32, tok(512)),
            ((b, n_heads, s, 64), BF16, head(n_heads, 64)),
            ((b, n_heads, s, 64), BF16, head(n_heads, 64)),
            ((b, s, 64 * n_groups), F32, tok(64 * n_groups)),
            ((b, s, 64 * n_groups), F32, tok(64 * n_groups)),
            ((b, n_groups, s, 64), BF16, head(n_groups, 64)),
            ((b, n_groups, 64, s), BF16,
             pl.BlockSpec((1, n_groups, 64, tm), lambda bi, si: (bi, 0, 0, si))),
            ((b, s, d), F32, tok(d)), ((b, s, d), F32, tok(d)), ((b, s, d), F32, tok(d))]
    kern = functools.partial(_in_kernel, n_heads, lw["mla_scale"], lw["diff_scale"])
    return pl.pallas_call(
        kern,
        grid=(b, s // tm),
        in_specs=in_specs,
        out_specs=[o[2] for o in outs],
        out_shape=[jax.ShapeDtypeStruct(o[0], o[1]) for o in outs],
        compiler_params=_cparams(("parallel", "parallel")),
        name="in_proj",
    )(x, sh, sc, lw["gn1"], tabs[0], tabs[1], tabs[2], *consts)


def _causal_pairs(nq, tq, tk):
    qi, ki = [], []
    for i in range(nq):
        for j in range((i * tq + tq - 1) // tk + 1):
            qi.append(i)
            ki.append(j)
    return jnp.asarray(np.array(qi, np.int32)), jnp.asarray(np.array(ki, np.int32))


def _flash_update_t(s_t, v_t, m_s, l_s, acc_s, idx):
    m_prev = m_s[idx]
    m_new = jnp.maximum(m_prev, jnp.max(s_t, axis=0, keepdims=True))
    alpha = jnp.exp2(m_prev - m_new)
    p_t = jnp.exp2(s_t - m_new)
    l_s[idx] = alpha * l_s[idx] + jnp.sum(p_t, axis=0, keepdims=True)
    acc_s[idx] = alpha * acc_s[idx] + _dot(v_t, p_t.astype(BF16))
    m_s[idx] = m_new


def _flash_init(m_s, l_s, acc_s):
    m_s[...] = jnp.full(m_s.shape, NEG, F32)
    l_s[...] = jnp.zeros(l_s.shape, F32)
    acc_s[...] = jnp.zeros(acc_s.shape, F32)


def _mla_prefill_kernel(n_heads, tq, tk, qi_tab, ki_tab, q_ref, k_ref, vt_ref, wuvt_ref, o_ref, m_s, l_s, acc_s):
    step = pl.program_id(1)
    qi, ki = qi_tab[step], ki_tab[step]
    last = (qi * tq + tq - 1) // tk

    @pl.when(ki == 0)
    def _():
        _flash_init(m_s, l_s, acc_s)

    def process(masked):
        k = k_ref[0]
        v_t = vt_ref[0]
        if masked:
            kpos = ki * tk + lax.broadcasted_iota(jnp.int32, (tk, tq), 0)
            qpos = qi * tq + lax.broadcasted_iota(jnp.int32, (tk, tq), 1)
            visible = kpos <= qpos

        def scores(hd):
            s_t = _nt_dot(k, q_ref[0, hd])
            return jnp.where(visible, s_t, NEG) if masked else s_t

        pending = [scores(hd) for hd in range(min(SCORE_AHEAD, n_heads))]
        for hd in range(n_heads):
            if hd + SCORE_AHEAD < n_heads:
                pending.append(scores(hd + SCORE_AHEAD))
            _flash_update_t(pending.pop(0), v_t, m_s, l_s, acc_s, hd)

    @pl.when(ki == last)
    def _():
        process(True)

    @pl.when(ki != last)
    def _():
        process(False)

    @pl.when(ki == last)
    def _():
        parts = [_dot(wuvt_ref[hd], (acc_s[hd] * (1.0 / l_s[hd])).astype(BF16)) for hd in range(n_heads)]
        o_ref[0] = jnp.concatenate(parts, axis=0).T.astype(o_ref.dtype)


def _mla_prefill(qcat, kcat, ckvt, wuvt):
    b, n_heads, s, _ = qcat.shape
    tq, tk = min(MLA_TQ, s), min(MLA_TK, s)
    assert s % tq == 0 and s % tk == 0
    qi_tab, ki_tab = _causal_pairs(s // tq, tq, tk)
    dv = wuvt.shape[1]
    grid_spec = pltpu.PrefetchScalarGridSpec(
        num_scalar_prefetch=2,
        grid=(b, int(qi_tab.shape[0])),
        in_specs=[pl.BlockSpec((1, n_heads, tq, 384), lambda bi, st, qt, kt: (bi, 0, qt[st], 0)),
                  pl.BlockSpec((1, tk, 384), lambda bi, st, qt, kt: (bi, kt[st], 0)),
                  pl.BlockSpec((1, 256, tk), lambda bi, st, qt, kt: (bi, 0, kt[st])),
                  pl.BlockSpec(wuvt.shape, lambda bi, st, qt, kt: (0, 0, 0))],
        out_specs=pl.BlockSpec((1, tq, dv * n_heads), lambda bi, st, qt, kt: (bi, qt[st], 0)),
        scratch_shapes=[pltpu.VMEM((n_heads, 1, tq), F32), pltpu.VMEM((n_heads, 1, tq), F32),
                        pltpu.VMEM((n_heads, 256, tq), F32)])
    return pl.pallas_call(
        functools.partial(_mla_prefill_kernel, n_heads, tq, tk),
        grid_spec=grid_spec,
        out_shape=jax.ShapeDtypeStruct((b, s, dv * n_heads), BF16),
        compiler_params=_cparams(("parallel", "arbitrary")),
        name="mla_prefill",
    )(qi_tab, ki_tab, qcat, kcat, ckvt, wuvt)


def _lambda(lam_ref, lam_init):
    a = lam_ref[...]
    t1 = jnp.sum(a[0:1] * a[1:2], axis=1, keepdims=True)
    t2 = jnp.sum(a[2:3] * a[3:4], axis=1, keepdims=True)
    return jnp.exp(t1) - jnp.exp(t2) + lam_init


def _diff_prefill_kernel(rep, td, lam_init, qi_tab, ki_tab, q1_ref, q2_ref, k_ref, vt_ref, dt_ref, pt_ref,
                         lam_ref, gs_ref, o_ref, bias_s, m_s, l_s, acc_s):
    step = pl.program_id(2)
    qi, ki = qi_tab[step], ki_tab[step]
    nb = td // BAND

    @pl.when(step == 0)
    def _():
        bias_s[...] = jnp.zeros(bias_s.shape, F32)
        for hd in range(rep):
            for i in range(nb):
                cols = slice(i * BAND, (i + 1) * BAND)
                bias_s[0, hd, i * BAND:(i + 1) * BAND, cols] = dt_ref[hd]
                if i > 0:
                    bias_s[0, hd, (i - 1) * BAND:i * BAND, cols] = pt_ref[hd]
                if i + 1 < nb:
                    bias_s[0, hd, (i + 1) * BAND:td, cols] = jnp.full((td - (i + 1) * BAND, BAND), NEG, F32)
            bias_s[1, hd, td - BAND:td, 0:BAND] = pt_ref[hd]

    @pl.when(ki == 0)
    def _():
        _flash_init(m_s, l_s, acc_s)

    def process(bias_idx):
        k = k_ref[0, 0]
        v_t = vt_ref[0, 0]
        def scores(unit):
            hd, q_ref = unit // 2, (q1_ref, q2_ref)[unit % 2]
            s_t = _nt_dot(k, q_ref[0, hd])
            return s_t if bias_idx is None else s_t + bias_s[bias_idx, hd]

        n_units = 2 * rep
        pending = [scores(u) for u in range(min(SCORE_AHEAD, n_units))]
        for unit in range(n_units):
            if unit + SCORE_AHEAD < n_units:
                pending.append(scores(unit + SCORE_AHEAD))
            _flash_update_t(pending.pop(0), v_t, m_s, l_s, acc_s, (unit % 2) * rep + unit // 2)

    @pl.when(ki == qi)
    def _():
        process(0)

    @pl.when(ki == qi - 1)
    def _():
        process(1)

    @pl.when(ki < qi - 1)
    def _():
        process(None)

    @pl.when(ki == qi)
    def _():
        lam = _lambda(lam_ref, lam_init)
        parts = []
        for hd in range(rep):
            o_t = (acc_s[hd] * (1.0 / l_s[hd]) - lam * (acc_s[rep + hd] * (1.0 / l_s[rep + hd])))
            ms = jnp.mean(o_t * o_t, axis=0, keepdims=True)
            parts.append(o_t * lax.rsqrt(ms + EPS) * gs_ref[...] * (1.0 - lam_init))
        o_ref[0] = jnp.concatenate(parts, axis=0).T.astype(o_ref.dtype)


def _diff_prefill(qd1, qd2, kd, vdt, dtile_t, ptile_t, lam_par, gsub_col, lam_init):
    b, n_heads, s, _ = qd1.shape
    n_groups = kd.shape[1]
    rep = n_heads // n_groups
    td = min(DIFF_TILE, s)
    assert td % BAND == 0 and s % td == 0
    qi_tab, ki_tab = _causal_pairs(s // td, td, td)
    qspec = pl.BlockSpec((1, rep, td, 64), lambda bi, g, st, qt, kt: (bi, g, qt[st], 0))
    tspec = pl.BlockSpec((rep, BAND, BAND), lambda bi, g, st, qt, kt: (g, 0, 0))
    grid_spec = pltpu.PrefetchScalarGridSpec(
        num_scalar_prefetch=2,
        grid=(b, n_groups, int(qi_tab.shape[0])),
        in_specs=[qspec, qspec,
                  pl.BlockSpec((1, 1, td, 64), lambda bi, g, st, qt, kt: (bi, g, kt[st], 0)),
                  pl.BlockSpec((1, 1, 64, td), lambda bi, g, st, qt, kt: (bi, g, 0, kt[st])),
                  tspec, tspec,
                  pl.BlockSpec(lam_par.shape, lambda bi, g, st, qt, kt: (0, 0)),
                  pl.BlockSpec((64, td), lambda bi, g, st, qt, kt: (0, 0))],
        out_specs=pl.BlockSpec((1, td, 64 * rep), lambda bi, g, st, qt, kt: (bi, qt[st], g)),
        scratch_shapes=[pltpu.VMEM((2, rep, td, td), F32), pltpu.VMEM((2 * rep, 1, td), F32),
                        pltpu.VMEM((2 * rep, 1, td), F32), pltpu.VMEM((2 * rep, 64, td), F32)])
    return pl.pallas_call(
        functools.partial(_diff_prefill_kernel, rep, td, lam_init),
        grid_spec=grid_spec,
        out_shape=jax.ShapeDtypeStruct((b, s, 64 * n_heads), BF16),
        compiler_params=_cparams(("parallel", "parallel", "arbitrary")),
        name="diff_prefill",
    )(qi_tab, ki_tab, qd1, qd2, kd, vdt, dtile_t, ptile_t, lam_par, gsub_col[:, :td])


def _page_copies(layer, cp, pt_ref, srcs, dsts, sems, seq, chunk, slot):
    out = []
    for j in range(cp):
        pg = pt_ref[seq, chunk * cp + j]
        for a, (src, dst) in enumerate(zip(srcs, dsts)):
            out.append(pltpu.make_async_copy(src.at[layer, pg], dst(slot, j), sems.at[a, slot]))
    return out


def _chunk_pipeline(layer, cp, n_chunks, pt_ref, srcs, dsts, sems):
    seq, chunk = pl.program_id(0), pl.program_id(1)
    gstep = seq * n_chunks + chunk
    total = pl.num_programs(0) * n_chunks
    slot = gstep % 2

    @pl.when(gstep == 0)
    def _():
        for c in _page_copies(layer, cp, pt_ref, srcs, dsts, sems, 0, 0, 0):
            c.start()

    @pl.when(gstep + 1 < total)
    def _():
        nxt = gstep + 1
        for c in _page_copies(layer, cp, pt_ref, srcs, dsts, sems, nxt // n_chunks, nxt % n_chunks, 1 - slot):
            c.start()

    for c in _page_copies(layer, cp, pt_ref, srcs, dsts, sems, seq, chunk, slot):
        c.wait()
    return slot


def _online_update(s, pv, m_s, l_s, acc_s):
    m_prev = m_s[...]
    m_new = jnp.maximum(m_prev, jnp.max(s, axis=1, keepdims=True))
    alpha = jnp.exp2(m_prev - m_new)
    p = jnp.exp2(s - m_new)
    l_s[...] = alpha * l_s[...] + jnp.sum(p, axis=1, keepdims=True)
    acc_s[...] = alpha * acc_s[...] + pv(p.astype(BF16))
    m_s[...] = m_new


def _mla_decode_kernel(layer, cp, n_chunks, n_heads, n_new, pt_ref, q_ref, knew_ref, wuv_ref, ckv_hbm, krt_hbm,
                       o_ref, ckv_buf, krt_buf, sems, m_s, l_s, acc_s):
    chunk = pl.program_id(1)
    page = ckv_buf.shape[2]
    dsts = (lambda sl, j: ckv_buf.at[sl, j], lambda sl, j: krt_buf.at[sl, :, pl.ds(j * page, page)])
    slot = _chunk_pipeline(layer, cp, n_chunks, pt_ref, (ckv_hbm, krt_hbm), dsts, sems)
    rows = q_ref.shape[1]

    @pl.when(chunk == 0)
    def _():
        _flash_init(m_s, l_s, acc_s)

    q = q_ref[0]
    kc = ckv_buf[slot].reshape(cp * page, 256).astype(BF16)
    kr_t = krt_buf[slot].astype(BF16)
    s = _nt_dot(q[:, 0:256], kc) + _dot(q[:, 256:288], kr_t)
    _online_update(s, lambda p: _dot(p, kc), m_s, l_s, acc_s)

    @pl.when(chunk == n_chunks - 1)
    def _():
        kn = knew_ref[0]
        sn = _nt_dot(q, kn)
        qidx = lax.broadcasted_iota(jnp.int32, sn.shape, 0) // n_heads
        kidx = lax.broadcasted_iota(jnp.int32, sn.shape, 1)
        sn = jnp.where((kidx <= qidx) & (kidx < n_new), sn, NEG)
        _online_update(sn, lambda p: _dot(p, kn[:, 0:256]), m_s, l_s, acc_s)
        o_lat = (acc_s[...] * (1.0 / l_s[...])).astype(BF16)
        full = _dot(o_lat, wuv_ref[...])
        hsel = (lax.broadcasted_iota(jnp.int32, (n_heads, full.shape[1]), 1) // 64
                == lax.broadcasted_iota(jnp.int32, (n_heads, full.shape[1]), 0))
        out_rows = [jnp.sum(jnp.where(hsel, full[i * n_heads:(i + 1) * n_heads], 0.0), axis=0, keepdims=True)
                    for i in range(rows // n_heads)]
        o_ref[0] = jnp.concatenate(out_rows, axis=0)


def _mla_decode(layer, page_table, qdec, knew, wuv_flat, cache_ckv, cache_krt, n_heads, n_new):
    bd, rows, _ = qdec.shape
    n_pages = page_table.shape[1]
    page = cache_ckv.shape[2]
    d_rope = cache_krt.shape[2]
    cp = min(DECODE_PAGES, n_pages)
    assert n_pages % cp == 0
    n_chunks = n_pages // cp
    grid_spec = pltpu.PrefetchScalarGridSpec(
        num_scalar_prefetch=1,
        grid=(bd, n_chunks),
        in_specs=[pl.BlockSpec((1, rows, 384), lambda b, c, pt: (b, 0, 0)),
                  pl.BlockSpec((1, 8, 384), lambda b, c, pt: (b, 0, 0)),
                  pl.BlockSpec(wuv_flat.shape, lambda b, c, pt: (0, 0)),
                  pl.BlockSpec(memory_space=pl.ANY),
                  pl.BlockSpec(memory_space=pl.ANY)],
        out_specs=pl.BlockSpec((1, rows // n_heads, 64 * n_heads), lambda b, c, pt: (b, 0, 0)),
        scratch_shapes=[pltpu.VMEM((2, cp, page, 256), F32), pltpu.VMEM((2, d_rope, cp * page), F32),
                        pltpu.SemaphoreType.DMA((2, 2)),
                        pltpu.VMEM((rows, 1), F32), pltpu.VMEM((rows, 1), F32), pltpu.VMEM((rows, 256), F32)])
    return pl.pallas_call(
        functools.partial(_mla_decode_kernel, layer, cp, n_chunks, n_heads, n_new),
        grid_spec=grid_spec,
        out_shape=jax.ShapeDtypeStruct((bd, rows // n_heads, 64 * n_heads), F32),
        compiler_params=_cparams(("arbitrary", "arbitrary")),
        name="mla_decode",
    )(page_table, qdec, knew, wuv_flat, cache_ckv, cache_krt)


def _diff_decode_kernel(layer, cp, n_chunks, n_heads, rep, n_new, lam_init, pt_ref, q_ref, knew_ref, vnew_ref,
                        bpast_ref, bnew_ref, lam_ref, gs_ref, kt_hbm, vt_hbm, o_ref, kt_buf, vt_buf, sems,
                        m_s, l_s, acc_s):
    chunk = pl.program_id(1)
    width = kt_buf.shape[1] * kt_buf.shape[2]
    ck = kt_buf.shape[3]
    page = ck // cp
    dsts = (lambda sl, j: kt_buf.at[sl, :, :, pl.ds(j * page, page)],
            lambda sl, j: vt_buf.at[sl, :, :, pl.ds(j * page, page)])
    slot = _chunk_pipeline(layer, cp, n_chunks, pt_ref, (kt_hbm, vt_hbm), dsts, sems)
    rows = q_ref.shape[1]
    half = rows // 2

    @pl.when(chunk == 0)
    def _():
        _flash_init(m_s, l_s, acc_s)

    q = q_ref[0]
    k_t = kt_buf[slot].reshape(width, ck).astype(BF16)
    v_t = vt_buf[slot].reshape(width, ck).astype(BF16)

    @pl.when(chunk < n_chunks - 1)
    def _():
        _online_update(_dot(q, k_t), lambda p: _nt_dot(p, v_t), m_s, l_s, acc_s)

    @pl.when(chunk == n_chunks - 1)
    def _():
        s = _dot(q, k_t)
        n_far = ck - BAND
        if n_far > 0:
            _online_update(s[:, 0:n_far], lambda p: _nt_dot(p, v_t[:, 0:n_far]), m_s, l_s, acc_s)
        _online_update(s[:, n_far:] + bpast_ref[...], lambda p: _nt_dot(p, v_t[:, n_far:]), m_s, l_s, acc_s)
        _online_update(_nt_dot(q, knew_ref[0]) + bnew_ref[...], lambda p: _dot(p, vnew_ref[0]), m_s, l_s, acc_s)
        lam = _lambda(lam_ref, lam_init)
        o = acc_s[...] * (1.0 / l_s[...])
        o = o[0:half] - lam * o[half:rows]
        hidx = lax.broadcasted_iota(jnp.int32, (half, 64), 0) % n_heads
        osel = o[:, 0:64]
        for g in range(1, n_heads // rep):
            osel = jnp.where(hidx // rep == g, o[:, 64 * g:64 * g + 64], osel)
        o_ref[0] = (_rms(osel, gs_ref[...]) * (1.0 - lam_init)).astype(o_ref.dtype)


def _diff_decode(layer, page_table, qdec, knew, vnew, bpast, bnew, lam_par, gsub, cache_kt, cache_vt,
                 n_heads, rep, n_new, lam_init):
    bd, rows, width = qdec.shape
    n_pages = page_table.shape[1]
    n_groups, dgrp, page = cache_kt.shape[2:]
    cp = min(DECODE_PAGES, n_pages)
    assert n_pages % cp == 0 and page == BAND and n_groups * dgrp == width
    n_chunks = n_pages // cp
    full2 = lambda a: pl.BlockSpec(a.shape, lambda b, c, pt: (0, 0))
    grid_spec = pltpu.PrefetchScalarGridSpec(
        num_scalar_prefetch=1,
        grid=(bd, n_chunks),
        in_specs=[pl.BlockSpec((1, rows, width), lambda b, c, pt: (b, 0, 0)),
                  pl.BlockSpec((1, 8, width), lambda b, c, pt: (b, 0, 0)),
                  pl.BlockSpec((1, 8, width), lambda b, c, pt: (b, 0, 0)),
                  full2(bpast), full2(bnew), full2(lam_par), full2(gsub),
                  pl.BlockSpec(memory_space=pl.ANY),
                  pl.BlockSpec(memory_space=pl.ANY)],
        out_specs=pl.BlockSpec((1, rows // 2, 64), lambda b, c, pt: (b, 0, 0)),
        scratch_shapes=[pltpu.VMEM((2, n_groups, dgrp, cp * page), F32),
                        pltpu.VMEM((2, n_groups, dgrp, cp * page), F32),
                        pltpu.SemaphoreType.DMA((2, 2)),
                        pltpu.VMEM((rows, 1), F32), pltpu.VMEM((rows, 1), F32), pltpu.VMEM((rows, width), F32)])
    return pl.pallas_call(
        functools.partial(_diff_decode_kernel, layer, cp, n_chunks, n_heads, rep, n_new, lam_init),
        grid_spec=grid_spec,
        out_shape=jax.ShapeDtypeStruct((bd, rows // 2, 64), BF16),
        compiler_params=_cparams(("arbitrary", "arbitrary")),
        name="diff_decode",
    )(page_table, qdec, knew, vnew, bpast, bnew, lam_par, gsub, cache_kt, cache_vt)


def _merge_kernel(x_ref, g1_ref, oa_ref, ub_ref, oc_ref, ga_ref, gb_ref, gc_ref,
                  woa_ref, wob_ref, woc_ref, wo_ref, o_ref):
    merged = (jax.nn.sigmoid(ga_ref[0]) * _dot(oa_ref[0], woa_ref[...])
              + jax.nn.sigmoid(gb_ref[0]) * _dot(ub_ref[0], wob_ref[...])
              + jax.nn.sigmoid(gc_ref[0]) * _dot(oc_ref[0], woc_ref[...]))
    o_ref[0] = x_ref[0] + g1_ref[0] * _dot(merged.astype(BF16), wo_ref[...])


def _merge(x, g1, oa, ub, oc, ga, gb, gc, lw):
    b, s, d = x.shape
    tm = min(TOKEN_TILE, s)
    tok = lambda width: pl.BlockSpec((1, tm, width), lambda bi, si: (bi, si, 0))
    consts = [lw[k] for k in ("woa", "wob", "woc", "wo")]
    return pl.pallas_call(
        _merge_kernel,
        grid=(b, s // tm),
        in_specs=[tok(d), _row_spec(g1, tm), tok(oa.shape[2]), tok(ub.shape[2]), tok(oc.shape[2]),
                  tok(d), tok(d), tok(d)] + [_const_spec(c.shape) for c in consts],
        out_specs=tok(d),
        out_shape=jax.ShapeDtypeStruct((b, s, d), F32),
        compiler_params=_cparams(("parallel", "parallel")),
        name="merge",
    )(x, g1, oa, ub, oc, ga, gb, gc, *consts)


def _ffn_kernel(seq_len, n_chunks, has_final, x_ref, halo_ref, sh_ref, sc_ref, g2_ref, gn_ref, p1_ref, p2_ref,
                wg_ref, wu_ref, wcv_ref, wd_ref, gf_ref, o_ref, u_ref):
    tm = x_ref.shape[1]
    ff = wg_ref.shape[1]
    tf = ff // n_chunks
    tu = u_ref.shape[1]
    x = x_ref[0]
    gn = gn_ref[...]
    h2 = (_rms(x, gn) * (1.0 + sc_ref[0]) + sh_ref[0]).astype(BF16)
    row = lax.broadcasted_iota(jnp.int32, (tm, 1), 0)
    if seq_len is None:
        sc_h = sc_ref[0][0:1] if sc_ref.shape[1] == 1 else sc_ref[0][0:8]
        sh_h = sh_ref[0][0:1] if sh_ref.shape[1] == 1 else sh_ref[0][0:8]
        live = (pl.program_id(1) > 0).astype(F32)
        hh = ((_rms(halo_ref[0], gn) * (1.0 + sc_h) + sh_h) * live).astype(BF16)
        pos = row
    else:
        pos = row % seq_len
    acc = jnp.zeros((tm, x.shape[1]), F32)
    for j in range(n_chunks):
        cols = slice(j * tf, (j + 1) * tf)
        u = _dot(h2, wg_ref[:, cols])
        if seq_len is None:
            uh = _dot(hh, wg_ref[:, cols])
            prev1 = uh[7:8]
            prev2 = jnp.where(row == 0, uh[6:7], uh[7:8])
        else:
            prev1 = p1_ref[0, :, cols]
            prev2 = p2_ref[0, :, cols]
        u1 = jnp.where(pos >= 1, pltpu.roll(u, 1, 0), prev1)
        u2 = jnp.where(pos >= 2, pltpu.roll(u, 2, 0), prev2)
        wcv = wcv_ref[:, cols]
        act = u2 * wcv[0:1] + u1 * wcv[1:2] + u * wcv[2:3]
        gated = (act * jax.nn.sigmoid(act)) * _dot(h2, wu_ref[:, cols])
        acc = acc + _dot(gated.astype(BF16), wd_ref[cols, :])
        u_ref[0, :, cols] = u[tm - tu:tm]
    y = x + g2_ref[0] * acc
    if has_final:
        y = _rms(y, gf_ref[...])
    o_ref[0] = y


def _ffn(x, sh, sc, g2, lw, prev, seq_len, g_final):
    b, s, d = x.shape
    ff = lw["wg"].shape[1]
    tm = min(TOKEN_TILE, s)
    n_chunks = 2 if (ff // 2) % LANES == 0 else 1
    tok = lambda width: pl.BlockSpec((1, tm, width), lambda bi, si: (bi, si, 0))
    if prev is None:
        tu = 8
        halo_spec = pl.BlockSpec((1, 8, d), lambda bi, si: (bi, jnp.maximum(si * (tm // 8) - 1, 0), 0))
        p1 = p2 = jnp.zeros((1, 8, LANES), F32)
        pspec = pl.BlockSpec((1, 8, LANES), lambda bi, si: (0, 0, 0))
    else:
        tu = tm
        halo_spec = pl.BlockSpec((1, 8, d), lambda bi, si: (bi, 0, 0))
        p1, p2 = prev
        pspec = tok(ff)
    has_final = g_final is not None
    gf = g_final.reshape(1, -1).astype(F32) if has_final else lw["gn2"]
    consts = [lw["wg"], lw["wu"], lw["wcv"], lw["wd"], gf]
    y, u = pl.pallas_call(
        functools.partial(_ffn_kernel, seq_len, n_chunks, has_final),
        grid=(b, s // tm),
        in_specs=[tok(d), halo_spec, _row_spec(sh, tm), _row_spec(sc, tm), _row_spec(g2, tm),
                  _const_spec(lw["gn2"].shape), pspec, pspec] + [_const_spec(c.shape) for c in consts],
        out_specs=[tok(d), pl.BlockSpec((1, tu, ff), lambda bi, si: (bi, si, 0))],
        out_shape=[jax.ShapeDtypeStruct((b, s, d), F32), jax.ShapeDtypeStruct((b, (s // tm) * tu, ff), F32)],
        compiler_params=_cparams(("parallel", "parallel")),
        name="ffn",
    )(x, x, sh, sc, g2, lw["gn2"], p1, p2, *consts)
    return y, u


def _rope_tables(pos, d_rope):
    half = d_rope // 2
    inv = ROPE_THETA ** (-jnp.arange(half, dtype=F32) / half)
    ang = pos.astype(F32)[:, None] * inv[None, :]
    cos, sin = jnp.cos(ang), jnp.sin(ang)
    pad = ((0, 0), (0, LANES - d_rope))
    c = jnp.pad(jnp.concatenate([cos, cos], axis=1), pad)
    sa = jnp.pad(jnp.concatenate([-sin, jnp.zeros_like(sin)], axis=1), pad)
    sb = jnp.pad(jnp.concatenate([jnp.zeros_like(sin), sin], axis=1), pad)
    return c, sa, sb


def _band_bias(rel_bias):
    n = jnp.arange(BAND, dtype=jnp.int32)
    max_exact = REL_BUCKETS // 2
    large = max_exact + (jnp.log(jnp.maximum(n, 1).astype(F32) / max_exact)
                         / math.log(REL_MAX_DIST / max_exact) * (REL_BUCKETS - max_exact)).astype(jnp.int32)
    bucket = jnp.where(n < max_exact, n, jnp.minimum(large, REL_BUCKETS - 1))
    tbl = rel_bias.astype(F32)
    return (tbl[bucket] - tbl[REL_BUCKETS - 1][None, :]).T


def _band_lookup(band, dist):
    val = band[:, jnp.clip(dist, 0, BAND - 1)]
    val = jnp.where(dist[None] >= BAND, 0.0, val)
    return jnp.where(dist[None] < 0, NEG, val)


def _layer_weights(l, p, sizes):
    offs = np.concatenate([[0], np.cumsum(sizes)])
    w = p["w_in"][l]
    seg = [w[:, offs[i]:offs[i + 1]].astype(BF16) for i in range(12)]
    d_rope = sizes[2]
    w_uq = p["w_uq"][l]
    d_nope = w_uq.shape[2] - d_rope
    row = lambda a: a.reshape(1, -1).astype(F32)
    lw = {
        "wq": seg[0], "wkv": seg[1], "wkr": jnp.pad(seg[2], ((0, 0), (0, LANES - d_rope))),
        "wb": seg[3], "wc": seg[4], "wx": seg[5], "wdq": seg[6], "wdk": seg[7], "wdv": seg[8],
        "wga": seg[9], "wgb": seg[10], "wgc": seg[11],
        "gn1": row(p["g_norm1"][l]), "gcq": row(p["g_cq"][l]), "gckv": row(p["g_ckv"][l]),
        "wuqn": jnp.transpose(w_uq[:, :, :d_nope], (1, 0, 2)).astype(BF16),
        "wuqr": jnp.pad(jnp.transpose(w_uq[:, :, d_nope:], (1, 0, 2)),
                        ((0, 0), (0, 0), (0, LANES - d_rope))).astype(BF16),
        "wukt": jnp.transpose(p["w_uk"][l], (1, 2, 0)).astype(BF16),
        "wuvt": jnp.transpose(p["w_uv"][l], (1, 2, 0)).astype(BF16),
        "wuv_flat": p["w_uv"][l].reshape(p["w_uv"].shape[1], -1).astype(BF16),
        "woa": p["w_oa"][l].astype(BF16), "wob": p["w_ob"][l].astype(BF16), "woc": p["w_oc"][l].astype(BF16),
        "wo": p["w_o"][l].astype(BF16),
        "wsc": p["w_sc"][l].astype(F32),
        "lam": jnp.stack([p["lam_q1"][l], p["lam_k1"][l], p["lam_q2"][l], p["lam_k2"][l]]).astype(F32),
        "gsub": row(p["g_subln"][l]),
        "gsub_col": jnp.broadcast_to(p["g_subln"][l].astype(F32)[:, None], (p["g_subln"].shape[1], DIFF_TILE)),
        "gn2": row(p["g_norm2"][l]),
        "wg": p["w_gate"][l].astype(BF16), "wu": p["w_up"][l].astype(BF16),
        "wcv": p["w_ffn_conv"][l].astype(F32), "wd": p["w_down"][l].astype(BF16),
        "mla_scale": float((d_nope + d_rope) ** -0.5 * LOG2E),
        "diff_scale": float((sizes[6] // w_uq.shape[1] // 2) ** -0.5 * LOG2E),
    }
    return lw


def _short_conv(v, prev, w):
    ext = jnp.concatenate([prev.astype(v.dtype), v], axis=1)
    s = v.shape[1]
    y = ext[:, 0:s] * w[0] + ext[:, 1:s + 1] * w[1] + ext[:, 2:s + 2] * w[2]
    return y, ext[:, s:s + 2]


def kernel(x_prompt, x_sample, c_prompt, c_sample, cache_mla_ckv, cache_mla_krope, cache_diff_k, cache_diff_v, state_shortconv, state_ffn_conv, page_table, rel_bias, w_ada, b_ada, g_norm1, w_in, g_cq, g_ckv, w_uq, w_uk, w_uv, w_oa, w_sc, w_ob, lam_q1, lam_k1, lam_q2, lam_k2, g_subln, w_oc, w_o, g_norm2, w_gate, w_up, w_ffn_conv, w_down, g_final):
    p = dict(w_in=w_in, g_norm1=g_norm1, g_cq=g_cq, g_ckv=g_ckv, w_uq=w_uq, w_uk=w_uk, w_uv=w_uv, w_oa=w_oa,
             w_sc=w_sc, w_ob=w_ob, lam_q1=lam_q1, lam_k1=lam_k1, lam_q2=lam_q2, lam_k2=lam_k2, g_subln=g_subln,
             w_oc=w_oc, w_o=w_o, g_norm2=g_norm2, w_gate=w_gate, w_up=w_up, w_ffn_conv=w_ffn_conv, w_down=w_down)
    n_layers = w_in.shape[0]
    bp, sp, d = x_prompt.shape
    bs, ss, _ = x_sample.shape
    n_heads = w_uq.shape[2]
    d_rope = cache_mla_krope.shape[-1]
    n_groups = cache_diff_k.shape[3]
    rep = n_heads // n_groups
    page = cache_mla_ckv.shape[2]
    past_len = page_table.shape[1] * page
    sizes = (w_uq.shape[1], cache_mla_ckv.shape[-1], d_rope, w_sc.shape[2], w_sc.shape[2], w_sc.shape[2],
             w_oc.shape[1], n_groups * cache_diff_k.shape[-1], n_groups * cache_diff_v.shape[-1], d, d, d)
    assert sum(sizes) == w_in.shape[2]
    page_table = page_table.astype(jnp.int32)

    n_c = bp + bs
    c_all = jnp.pad(jnp.concatenate([c_prompt, c_sample], axis=0), ((0, (-n_c) % 8), (0, 0)))
    mod = _adaln(c_all, w_ada, b_ada)

    tabs_p = _rope_tables(jnp.arange(sp, dtype=jnp.int32), d_rope)
    tabs_s = _rope_tables(past_len + (jnp.arange(bs * ss, dtype=jnp.int32) % ss), d_rope)
    band = _band_bias(rel_bias) * LOG2E
    ar = jnp.arange(BAND, dtype=jnp.int32)
    dtile_t = _band_lookup(band, ar[None, :] - ar[:, None])
    ptile_t = _band_lookup(band, BAND + ar[None, :] - ar[:, None])
    qi_rows = jnp.repeat(jnp.arange(ss, dtype=jnp.int32), n_heads)
    bpast = _band_lookup(band, BAND + qi_rows[:, None] - ar[None, :])
    new_idx = jnp.arange(8, dtype=jnp.int32)
    dist_new = jnp.where(new_idx[None, :] < ss, qi_rows[:, None] - new_idx[None, :], -1)
    bnew = _band_lookup(band, dist_new)
    hrow = jnp.tile(jnp.arange(n_heads), ss)
    pick = lambda a: jnp.tile(a[hrow, jnp.arange(ss * n_heads)], (2, 1))
    bpast, bnew = pick(bpast), pick(bnew)

    cache_krt = jnp.transpose(cache_mla_krope, (0, 1, 3, 2))
    cache_dkt = jnp.transpose(cache_diff_k, (0, 1, 3, 4, 2))
    cache_dvt = jnp.transpose(cache_diff_v, (0, 1, 3, 4, 2))

    xp, xs = x_prompt, x_sample.reshape(1, bs * ss, d)
    p_new, s_new = [], []
    for l in range(n_layers):
        lw = _layer_weights(l, p, sizes)
        lam_init = 0.8 - 0.6 * math.exp(-0.3 * l)
        mods_p = [m[:, None, :] for m in jnp.split(mod[l, :bp], 6, axis=-1)]
        mods_s = [jnp.repeat(m, ss, axis=0)[None] for m in jnp.split(mod[l, bp:n_c], 6, axis=-1)]

        sh1, sc1, g1, sh2, sc2, g2 = mods_p
        (qcat, ckv, kr, kcat, ckvt, zb, v, qd1, qd2, dk, dv, kd, vdt, ga, gb, gc) = _in_proj(xp, sh1, sc1, lw, tabs_p)
        oa = _mla_prefill(qcat, kcat, ckvt, lw["wuvt"])
        oc = _diff_prefill(qd1, qd2, kd, vdt, dtile_t, ptile_t, lw["lam"], lw["gsub_col"], lam_init)
        y_sc, sc_state = _short_conv(v, jnp.zeros((bp, 2, v.shape[2]), F32), lw["wsc"])
        xp = _merge(xp, g1, oa, (zb * y_sc).astype(BF16), oc, ga, gb, gc, lw)
        xp, u_tail = _ffn(xp, sh2, sc2, g2, lw, None, None, g_final if l == n_layers - 1 else None)
        p_new.append((ckv, kr, dk.reshape(bp, sp, n_groups, -1), dv.reshape(bp, sp, n_groups, -1),
                      sc_state, u_tail[:, -2:]))

        sh1, sc1, g1, sh2, sc2, g2 = mods_s
        (qcat, ckv, kr, kcat, ckvt, zb, v, qd1, qd2, dk, dv, kd, vdt, ga, gb, gc) = _in_proj(xs, sh1, sc1, lw, tabs_s)
        pad_new = lambda a: jnp.pad(a.reshape(bs, ss, a.shape[-1]), ((0, 0), (0, 8 - ss), (0, 0)))
        qdec = jnp.transpose(qcat[0].reshape(n_heads, bs, ss, 384), (1, 2, 0, 3)).reshape(bs, ss * n_heads, 384)
        oa = _mla_decode(l, page_table, qdec, pad_new(kcat[0]), lw["wuv_flat"], cache_mla_ckv, cache_krt,
                         n_heads, ss)
        def dec_rows(qd):
            a = jnp.transpose(qd[0].reshape(n_groups, rep, bs, ss, 64), (2, 3, 0, 1, 4))
            a = a[:, :, :, :, None, :] * jnp.eye(n_groups, dtype=a.dtype)[None, None, :, None, :, None]
            return a.reshape(bs, ss * n_heads, n_groups * 64)
        qd_dec = jnp.concatenate([dec_rows(qd1), dec_rows(qd2)], axis=1)
        oc = _diff_decode(l, page_table, qd_dec, pad_new(dk[0].astype(BF16)), pad_new(dv[0].astype(BF16)),
                          bpast, bnew, lw["lam"], lw["gsub"], cache_dkt, cache_dvt, n_heads, rep, ss, lam_init)
        oa = oa.reshape(1, bs * ss, -1).astype(BF16)
        oc = oc.reshape(1, bs * ss, -1)
        y_sc, sc_state = _short_conv(v.reshape(bs, ss, -1), state_shortconv[l], lw["wsc"])
        ub = (zb * y_sc.reshape(1, bs * ss, -1)).astype(BF16)
        xs = _merge(xs, g1, oa, ub, oc, ga, gb, gc, lw)
        st = state_ffn_conv[l].astype(F32)
        zero = jnp.zeros((bs, ss, st.shape[2]), F32)
        p1 = zero.at[:, 0].set(st[:, 1]).reshape(1, bs * ss, -1)
        p2 = zero.at[:, 0].set(st[:, 0]).at[:, 1].set(st[:, 1]).reshape(1, bs * ss, -1)
        xs, u_all = _ffn(xs, sh2, sc2, g2, lw, (p1, p2), ss, g_final if l == n_layers - 1 else None)
        s_new.append((ckv.reshape(bs, ss, -1), kr.reshape(bs, ss, -1), dk.reshape(bs, ss, n_groups, -1),
                      dv.reshape(bs, ss, n_groups, -1), sc_state, u_all.reshape(bs, ss, -1)[:, ss - 2:]))

    stack = lambda lst, i: jnp.stack([e[i] for e in lst], axis=0)
    return ((xp, xs.reshape(bs, ss, d)) + tuple(stack(p_new, i) for i in range(6))
            + tuple(stack(s_new, i) for i in range(6)))
```

```python
import functools
import math

import jax
import jax.numpy as jnp
import numpy as np
from jax import lax
from jax.experimental import pallas as pl
from jax.experimental.pallas import tpu as pltpu

F32 = jnp.float32
BF16 = jnp.bfloat16

EPS = 1e-6
NEG = -1e30
ROPE_THETA = 10000.0
REL_BUCKETS = 32
REL_MAX_DIST = 128
LANES = 128
BAND = 128
VMEM_LIMIT = 56 * 1024 * 1024
LOG2E = 1.4426950408889634

TOKEN_TILE = 256
MLA_TQ, MLA_TK = 256, 512
DIFF_TILE = 512
DIFF_QCHUNK = 256
DECODE_PAGES = 64
SCORE_AHEAD = 3


def _cparams(sem):
    return pltpu.CompilerParams(dimension_semantics=sem, vmem_limit_bytes=VMEM_LIMIT)


def _const_spec(shape):
    nd = len(shape)
    return pl.BlockSpec(tuple(shape), lambda *_: (0,) * nd, pipeline_mode=pl.Buffered(1))


def _row_spec(arr, tm):
    if arr.shape[1] == 1:
        return pl.BlockSpec((1, 1, arr.shape[2]), lambda b, s: (b, 0, 0))
    return pl.BlockSpec((1, tm, arr.shape[2]), lambda b, s: (b, s, 0))


def _rms(x, g):
    return x * lax.rsqrt(jnp.mean(x * x, axis=-1, keepdims=True) + EPS) * g


def _nt_dot(a, b):
    return lax.dot_general(a, b, (((1,), (1,)), ((), ())), preferred_element_type=F32)


def _dot(a, b):
    return jnp.dot(a, b, preferred_element_type=F32)


def _ada_kernel(c_ref, w_ref, b_ref, o_ref):
    c = c_ref[...]
    a = (c * jax.nn.sigmoid(c)).astype(BF16)
    o_ref[0] = _dot(a, w_ref[0].astype(BF16)) + b_ref[0]


def _adaln(c_all, w_ada, b_ada):
    n_layers, d, n = w_ada.shape
    m = c_all.shape[0]
    tn = 1024 if n % 1024 == 0 else n
    return pl.pallas_call(
        _ada_kernel,
        grid=(n_layers, n // tn),
        in_specs=[pl.BlockSpec((m, d), lambda l, j: (0, 0)),
                  pl.BlockSpec((1, d, tn), lambda l, j: (l, 0, j)),
                  pl.BlockSpec((1, 1, tn), lambda l, j: (l, 0, j))],
        out_specs=pl.BlockSpec((1, m, tn), lambda l, j: (l, 0, j)),
        out_shape=jax.ShapeDtypeStruct((n_layers, m, n), F32),
        compiler_params=_cparams(("parallel", "parallel")),
        name="adaln",
    )(c_all, w_ada, b_ada.reshape(n_layers, 1, n))


def _rope128(x, c, sa, sb):
    return x * c + pltpu.roll(x, LANES - 16, 1) * sa + pltpu.roll(x, 16, 1) * sb


def _in_kernel(n_heads, mla_scale, diff_scale,
               x_ref, sh_ref, sc_ref, gn_ref, cos_ref, sa_ref, sb_ref,
               wq_ref, wkv_ref, wkr_ref, wb_ref, wc_ref, wx_ref, wdq_ref, wdk_ref, wdv_ref,
               wga_ref, wgb_ref, wgc_ref, gcq_ref, gckv_ref, wuqn_ref, wuqr_ref, wukt_ref,
               qcat_o, ckv_o, kr_o, kcat_o, ckvt_o, zb_o, v_o, qd1_o, qd2_o, dk_o, dv_o, kd_o, vdt_o,
               ga_o, gb_o, gc_o):
    x = x_ref[0]
    h = _rms(x, gn_ref[...]) * (1.0 + sc_ref[0]) + sh_ref[0]
    hb = h.astype(BF16)
    cos, sa, sb = cos_ref[...], sa_ref[...], sb_ref[...]

    cqb = _rms(_dot(hb, wq_ref[...]), gcq_ref[...]).astype(BF16)
    for hd in range(n_heads):
        qn = _dot(cqb, wuqn_ref[hd]).astype(BF16)
        qcat_o[0, hd, :, 0:256] = (_dot(qn, wukt_ref[hd]) * mla_scale).astype(BF16)
        qr = _rope128(_dot(cqb, wuqr_ref[hd]), cos, sa, sb)
        qcat_o[0, hd, :, 256:384] = (qr * mla_scale).astype(BF16)

    ckv = _rms(_dot(hb, wkv_ref[...]), gckv_ref[...])
    ckv_o[0] = ckv
    kr = _rope128(_dot(hb, wkr_ref[...]), cos, sa, sb)
    kr_o[0] = kr[:, 0:32]
    kcat_o[0, :, 0:256] = ckv.astype(BF16)
    kcat_o[0, :, 256:384] = kr.astype(BF16)
    ckvt_o[0] = ckv.T.astype(BF16)

    zb_o[0] = _dot(hb, wb_ref[...])
    v_o[0] = _dot(hb, wc_ref[...]) * _dot(hb, wx_ref[...])

    dq = _dot(hb, wdq_ref[...]) * diff_scale
    lane = lax.broadcasted_iota(jnp.int32, (dq.shape[0], 64), 1)
    for hd in range(n_heads):
        sl = dq[:, 64 * hd:64 * hd + 64]
        qd1_o[0, hd] = jnp.where(lane < 32, sl, 0.0).astype(BF16)
        qd2_o[0, hd] = jnp.where(lane >= 32, sl, 0.0).astype(BF16)
    dk = _dot(hb, wdk_ref[...])
    dv = _dot(hb, wdv_ref[...])
    dk_o[0] = dk
    dv_o[0] = dv
    dvt = dv.T
    for g in range(dk.shape[1] // 64):
        kd_o[0, g] = dk[:, 64 * g:64 * g + 64].astype(BF16)
        vdt_o[0, g] = dvt[64 * g:64 * g + 64].astype(BF16)

    ga_o[0] = _dot(hb, wga_ref[...])
    gb_o[0] = _dot(hb, wgb_ref[...])
    gc_o[0] = _dot(hb, wgc_ref[...])


def _in_proj(x, sh, sc, lw, tabs):
    b, s, d = x.shape
    n_heads = lw["wuqn"].shape[0]
    n_groups = lw["wdk"].shape[1] // 64
    tm = min(TOKEN_TILE, s)
    consts = [lw[k] for k in ("wq", "wkv", "wkr", "wb", "wc", "wx", "wdq", "wdk", "wdv", "wga", "wgb", "wgc",
                              "gcq", "gckv", "wuqn", "wuqr", "wukt")]
    tok = lambda width: pl.BlockSpec((1, tm, width), lambda bi, si: (bi, si, 0))
    head = lambda nh, width: pl.BlockSpec((1, nh, tm, width), lambda bi, si: (bi, 0, si, 0))
    tab_spec = pl.BlockSpec((tm, LANES), lambda bi, si: (si, 0))
    in_specs = ([tok(d), _row_spec(sh, tm), _row_spec(sc, tm), _const_spec(lw["gn1"].shape),
                 tab_spec, tab_spec, tab_spec] + [_const_spec(c.shape) for c in consts])
    outs = [((b, n_heads, s, 384), BF16, head(n_heads, 384)),
            ((b, s, 256), F32, tok(256)),
            ((b, s, 32), F32, tok(32)),
            ((b, s, 384), BF16, tok(384)),
            ((b, 256, s), BF16, pl.BlockSpec((1, 256, tm), lambda bi, si: (bi, 0, si))),
            ((b, s, 512), F32, tok(512)),
            ((b, s, 512), F32, tok(512)),
            ((b, n_heads, s, 64), BF16, head(n_heads, 64)),
            ((b, n_heads, s, 64), BF16, head(n_heads, 64)),
            ((b, s, 64 * n_groups), F32, tok(64 * n_groups)),
            ((b, s, 64 * n_groups), F32, tok(64 * n_groups)),
            ((b, n_groups, s, 64), BF16, head(n_groups, 64)),
            ((b, n_groups, 64, s), BF16,
             pl.BlockSpec((1, n_groups, 64, tm), lambda bi, si: (bi, 0, 0, si))),
            ((b, s, d), F32, tok(d)), ((b, s, d), F32, tok(d)), ((b, s, d), F32, tok(d))]
    kern = functools.partial(_in_kernel, n_heads, lw["mla_scale"], lw["diff_scale"])
    return pl.pallas_call(
        kern,
        grid=(b, s // tm),
        in_specs=in_specs,
        out_specs=[o[2] for o in outs],
        out_shape=[jax.ShapeDtypeStruct(o[0], o[1]) for o in outs],
        compiler_params=_cparams(("parallel", "parallel")),
        name="in_proj",
    )(x, sh, sc, lw["gn1"], tabs[0], tabs[1], tabs[2], *consts)


def _causal_pairs(nq, tq, tk):
    qi, ki = [], []
    for i in range(nq):
        for j in range((i * tq + tq - 1) // tk + 1):
            qi.append(i)
            ki.append(j)
    return jnp.asarray(np.array(qi, np.int32)), jnp.asarray(np.array(ki, np.int32))


def _flash_update_t(s_t, v_t, m_s, l_s, acc_s, idx):
    m_prev = m_s[idx]
    m_new = jnp.maximum(m_prev, jnp.max(s_t, axis=0, keepdims=True))
    alpha = jnp.exp2(m_prev - m_new)
    p_t = jnp.exp2(s_t - m_new)
    l_s[idx] = alpha * l_s[idx] + jnp.sum(p_t, axis=0, keepdims=True)
    acc_s[idx] = alpha * acc_s[idx] + _dot(v_t, p_t.astype(BF16))
    m_s[idx] = m_new


def _flash_init(m_s, l_s, acc_s):
    m_s[...] = jnp.full(m_s.shape, NEG, F32)
    l_s[...] = jnp.zeros(l_s.shape, F32)
    acc_s[...] = jnp.zeros(acc_s.shape, F32)


def _mla_prefill_kernel(n_heads, tq, tk, qi_tab, ki_tab, q_ref, k_ref, vt_ref, wuvt_ref, o_ref, m_s, l_s, acc_s):
    step = pl.program_id(1)
    qi, ki = qi_tab[step], ki_tab[step]
    last = (qi * tq + tq - 1) // tk

    @pl.when(ki == 0)
    def _():
        _flash_init(m_s, l_s, acc_s)

    def process(masked):
        k = k_ref[0]
        v_t = vt_ref[0]
        if masked:
            kpos = ki * tk + lax.broadcasted_iota(jnp.int32, (tk, tq), 0)
            qpos = qi * tq + lax.broadcasted_iota(jnp.int32, (tk, tq), 1)
            visible = kpos <= qpos

        def scores(hd):
            s_t = _nt_dot(k, q_ref[0, hd])
            return jnp.where(visible, s_t, NEG) if masked else s_t

        pending = [scores(hd) for hd in range(min(SCORE_AHEAD, n_heads))]
        for hd in range(n_heads):
            if hd + SCORE_AHEAD < n_heads:
                pending.append(scores(hd + SCORE_AHEAD))
            _flash_update_t(pending.pop(0), v_t, m_s, l_s, acc_s, hd)

    @pl.when(ki == last)
    def _():
        process(True)

    @pl.when(ki != last)
    def _():
        process(False)

    @pl.when(ki == last)
    def _():
        parts = [_dot(wuvt_ref[hd], (acc_s[hd] * (1.0 / l_s[hd])).astype(BF16)) for hd in range(n_heads)]
        o_ref[0] = jnp.concatenate(parts, axis=0).T.astype(o_ref.dtype)


def _mla_prefill(qcat, kcat, ckvt, wuvt):
    b, n_heads, s, _ = qcat.shape
    tq, tk = min(MLA_TQ, s), min(MLA_TK, s)
    assert s % tq == 0 and s % tk == 0
    qi_tab, ki_tab = _causal_pairs(s // tq, tq, tk)
    dv = wuvt.shape[1]
    grid_spec = pltpu.PrefetchScalarGridSpec(
        num_scalar_prefetch=2,
        grid=(b, int(qi_tab.shape[0])),
        in_specs=[pl.BlockSpec((1, n_heads, tq, 384), lambda bi, st, qt, kt: (bi, 0, qt[st], 0)),
                  pl.BlockSpec((1, tk, 384), lambda bi, st, qt, kt: (bi, kt[st], 0)),
                  pl.BlockSpec((1, 256, tk), lambda bi, st, qt, kt: (bi, 0, kt[st])),
                  pl.BlockSpec(wuvt.shape, lambda bi, st, qt, kt: (0, 0, 0))],
        out_specs=pl.BlockSpec((1, tq, dv * n_heads), lambda bi, st, qt, kt: (bi, qt[st], 0)),
        scratch_shapes=[pltpu.VMEM((n_heads, 1, tq), F32), pltpu.VMEM((n_heads, 1, tq), F32),
                        pltpu.VMEM((n_heads, 256, tq), F32)])
    return pl.pallas_call(
        functools.partial(_mla_prefill_kernel, n_heads, tq, tk),
        grid_spec=grid_spec,
        out_shape=jax.ShapeDtypeStruct((b, s, dv * n_heads), BF16),
        compiler_params=_cparams(("parallel", "arbitrary")),
        name="mla_prefill",
    )(qi_tab, ki_tab, qcat, kcat, ckvt, wuvt)


def _lambda(lam_ref, lam_init):
    a = lam_ref[...]
    t1 = jnp.sum(a[0:1] * a[1:2], axis=1, keepdims=True)
    t2 = jnp.sum(a[2:3] * a[3:4], axis=1, keepdims=True)
    return jnp.exp(t1) - jnp.exp(t2) + lam_init


def _diff_prefill_kernel(rep, td, lam_init, qi_tab, ki_tab, q1_ref, q2_ref, k_ref, vt_ref, dt_ref, pt_ref,
                         lam_ref, gs_ref, o_ref, bias_s, m_s, l_s, acc_s):
    step = pl.program_id(2)
    qi, ki = qi_tab[step], ki_tab[step]
    nb = td // BAND

    @pl.when(step == 0)
    def _():
        bias_s[...] = jnp.zeros(bias_s.shape, F32)
        for hd in range(rep):
            for i in range(nb):
                cols = slice(i * BAND, (i + 1) * BAND)
                bias_s[0, hd, i * BAND:(i + 1) * BAND, cols] = dt_ref[hd]
                if i > 0:
                    bias_s[0, hd, (i - 1) * BAND:i * BAND, cols] = pt_ref[hd]
                if i + 1 < nb:
                    bias_s[0, hd, (i + 1) * BAND:td, cols] = jnp.full((td - (i + 1) * BAND, BAND), NEG, F32)
            bias_s[1, hd, td - BAND:td, 0:BAND] = pt_ref[hd]

    @pl.when(ki == 0)
    def _():
        _flash_init(m_s, l_s, acc_s)

    tqc = m_s.shape[2]
    nqc = td // tqc

    def process(bias_idx):
        def n_keys(unit):
            return (unit % nqc + 1) * tqc if bias_idx == 0 else td

        def scores(unit):
            hd, mp, qc = unit // (2 * nqc), (unit // nqc) % 2, unit % nqc
            q_ref = (q1_ref, q2_ref)[mp]
            nk = n_keys(unit)
            s_t = _nt_dot(k_ref[0, 0, 0:nk], q_ref[0, hd, qc * tqc:(qc + 1) * tqc])
            if bias_idx == 0 or (bias_idx == 1 and qc == 0):
                s_t = s_t + bias_s[bias_idx, hd, 0:nk, qc * tqc:(qc + 1) * tqc]
            return s_t

        n_units = 2 * rep * nqc
        pending = [scores(u) for u in range(min(SCORE_AHEAD, n_units))]
        for unit in range(n_units):
            if unit + SCORE_AHEAD < n_units:
                pending.append(scores(unit + SCORE_AHEAD))
            _flash_update_t(pending.pop(0), vt_ref[0, 0, :, 0:n_keys(unit)], m_s, l_s, acc_s, unit)

    @pl.when(ki == qi)
    def _():
        process(0)

    @pl.when(ki == qi - 1)
    def _():
        process(1)

    @pl.when(ki < qi - 1)
    def _():
        process(None)

    @pl.when(ki == qi)
    def _():
        lam = _lambda(lam_ref, lam_init)
        for qc in range(nqc):
            parts = []
            for hd in range(rep):
                u1, u2 = (2 * hd) * nqc + qc, (2 * hd + 1) * nqc + qc
                o_t = acc_s[u1] * (1.0 / l_s[u1]) - lam * (acc_s[u2] * (1.0 / l_s[u2]))
                ms = jnp.mean(o_t * o_t, axis=0, keepdims=True)
                parts.append(o_t * lax.rsqrt(ms + EPS) * gs_ref[:, 0:tqc] * (1.0 - lam_init))
            o_ref[0, qc * tqc:(qc + 1) * tqc, :] = jnp.concatenate(parts, axis=0).T.astype(o_ref.dtype)


def _diff_prefill(qd1, qd2, kd, vdt, dtile_t, ptile_t, lam_par, gsub_col, lam_init):
    b, n_heads, s, _ = qd1.shape
    n_groups = kd.shape[1]
    rep = n_heads // n_groups
    td = min(DIFF_TILE, s)
    tqc = min(DIFF_QCHUNK, td)
    assert td % BAND == 0 and s % td == 0 and td % tqc == 0
    n_units = 2 * rep * (td // tqc)
    qi_tab, ki_tab = _causal_pairs(s // td, td, td)
    qspec = pl.BlockSpec((1, rep, td, 64), lambda bi, g, st, qt, kt: (bi, g, qt[st], 0))
    tspec = pl.BlockSpec((rep, BAND, BAND), lambda bi, g, st, qt, kt: (g, 0, 0))
    grid_spec = pltpu.PrefetchScalarGridSpec(
        num_scalar_prefetch=2,
        grid=(b, n_groups, int(qi_tab.shape[0])),
        in_specs=[qspec, qspec,
                  pl.BlockSpec((1, 1, td, 64), lambda bi, g, st, qt, kt: (bi, g, kt[st], 0)),
                  pl.BlockSpec((1, 1, 64, td), lambda bi, g, st, qt, kt: (bi, g, 0, kt[st])),
                  tspec, tspec,
                  pl.BlockSpec(lam_par.shape, lambda bi, g, st, qt, kt: (0, 0)),
                  pl.BlockSpec((64, td), lambda bi, g, st, qt, kt: (0, 0))],
        out_specs=pl.BlockSpec((1, td, 64 * rep), lambda bi, g, st, qt, kt: (bi, qt[st], g)),
        scratch_shapes=[pltpu.VMEM((2, rep, td, td), F32), pltpu.VMEM((n_units, 1, tqc), F32),
                        pltpu.VMEM((n_units, 1, tqc), F32), pltpu.VMEM((n_units, 64, tqc), F32)])
    return pl.pallas_call(
        functools.partial(_diff_prefill_kernel, rep, td, lam_init),
        grid_spec=grid_spec,
        out_shape=jax.ShapeDtypeStruct((b, s, 64 * n_heads), BF16),
        compiler_params=_cparams(("parallel", "parallel", "arbitrary")),
        name="diff_prefill",
    )(qi_tab, ki_tab, qd1, qd2, kd, vdt, dtile_t, ptile_t, lam_par, gsub_col[:, :td])


def _page_copies(layer, cp, pt_ref, srcs, dsts, sems, seq, chunk, slot):
    out = []
    for j in range(cp):
        pg = pt_ref[seq, chunk * cp + j]
        for a, (src, dst) in enumerate(zip(srcs, dsts)):
            out.append(pltpu.make_async_copy(src.at[layer, pg], dst(slot, j), sems.at[a, slot]))
    return out


def _chunk_pipeline(layer, cp, n_chunks, pt_ref, srcs, dsts, sems):
    seq, chunk = pl.program_id(0), pl.program_id(1)
    gstep = seq * n_chunks + chunk
    total = pl.num_programs(0) * n_chunks
    slot = gstep % 2

    @pl.when(gstep == 0)
    def _():
        for c in _page_copies(layer, cp, pt_ref, srcs, dsts, sems, 0, 0, 0):
            c.start()

    @pl.when(gstep + 1 < total)
    def _():
        nxt = gstep + 1
        for c in _page_copies(layer, cp, pt_ref, srcs, dsts, sems, nxt // n_chunks, nxt % n_chunks, 1 - slot):
            c.start()

    for c in _page_copies(layer, cp, pt_ref, srcs, dsts, sems, seq, chunk, slot):
        c.wait()
    return slot


def _online_update(s, pv, m_s, l_s, acc_s):
    m_prev = m_s[...]
    m_new = jnp.maximum(m_prev, jnp.max(s, axis=1, keepdims=True))
    alpha = jnp.exp2(m_prev - m_new)
    p = jnp.exp2(s - m_new)
    l_s[...] = alpha * l_s[...] + jnp.sum(p, axis=1, keepdims=True)
    acc_s[...] = alpha * acc_s[...] + pv(p.astype(BF16))
    m_s[...] = m_new


def _mla_decode_kernel(layer, cp, n_chunks, n_heads, n_new, pt_ref, q_ref, knew_ref, wuv_ref, ckv_hbm, krt_hbm,
                       o_ref, ckv_buf, krt_buf, sems, m_s, l_s, acc_s):
    chunk = pl.program_id(1)
    page = ckv_buf.shape[2]
    dsts = (lambda sl, j: ckv_buf.at[sl, j], lambda sl, j: krt_buf.at[sl, :, pl.ds(j * page, page)])
    slot = _chunk_pipeline(layer, cp, n_chunks, pt_ref, (ckv_hbm, krt_hbm), dsts, sems)
    rows = q_ref.shape[1]

    @pl.when(chunk == 0)
    def _():
        _flash_init(m_s, l_s, acc_s)

    q = q_ref[0]
    kc = ckv_buf[slot].reshape(cp * page, 256).astype(BF16)
    kr_t = krt_buf[slot].astype(BF16)
    s = _nt_dot(q[:, 0:256], kc) + _dot(q[:, 256:288], kr_t)
    _online_update(s, lambda p: _dot(p, kc), m_s, l_s, acc_s)

    @pl.when(chunk == n_chunks - 1)
    def _():
        kn = knew_ref[0]
        sn = _nt_dot(q, kn)
        qidx = lax.broadcasted_iota(jnp.int32, sn.shape, 0) // n_heads
        kidx = lax.broadcasted_iota(jnp.int32, sn.shape, 1)
        sn = jnp.where((kidx <= qidx) & (kidx < n_new), sn, NEG)
        _online_update(sn, lambda p: _dot(p, kn[:, 0:256]), m_s, l_s, acc_s)
        o_lat = (acc_s[...] * (1.0 / l_s[...])).astype(BF16)
        full = _dot(o_lat, wuv_ref[...])
        hsel = (lax.broadcasted_iota(jnp.int32, (n_heads, full.shape[1]), 1) // 64
                == lax.broadcasted_iota(jnp.int32, (n_heads, full.shape[1]), 0))
        out_rows = [jnp.sum(jnp.where(hsel, full[i * n_heads:(i + 1) * n_heads], 0.0), axis=0, keepdims=True)
                    for i in range(rows // n_heads)]
        o_ref[0] = jnp.concatenate(out_rows, axis=0)


def _mla_decode(layer, page_table, qdec, knew, wuv_flat, cache_ckv, cache_krt, n_heads, n_new):
    bd, rows, _ = qdec.shape
    n_pages = page_table.shape[1]
    page = cache_ckv.shape[2]
    d_rope = cache_krt.shape[2]
    cp = min(DECODE_PAGES, n_pages)
    assert n_pages % cp == 0
    n_chunks = n_pages // cp
    grid_spec = pltpu.PrefetchScalarGridSpec(
        num_scalar_prefetch=1,
        grid=(bd, n_chunks),
        in_specs=[pl.BlockSpec((1, rows, 384), lambda b, c, pt: (b, 0, 0)),
                  pl.BlockSpec((1, 8, 384), lambda b, c, pt: (b, 0, 0)),
                  pl.BlockSpec(wuv_flat.shape, lambda b, c, pt: (0, 0)),
                  pl.BlockSpec(memory_space=pl.ANY),
                  pl.BlockSpec(memory_space=pl.ANY)],
        out_specs=pl.BlockSpec((1, rows // n_heads, 64 * n_heads), lambda b, c, pt: (b, 0, 0)),
        scratch_shapes=[pltpu.VMEM((2, cp, page, 256), F32), pltpu.VMEM((2, d_rope, cp * page), F32),
                        pltpu.SemaphoreType.DMA((2, 2)),
                        pltpu.VMEM((rows, 1), F32), pltpu.VMEM((rows, 1), F32), pltpu.VMEM((rows, 256), F32)])
    return pl.pallas_call(
        functools.partial(_mla_decode_kernel, layer, cp, n_chunks, n_heads, n_new),
        grid_spec=grid_spec,
        out_shape=jax.ShapeDtypeStruct((bd, rows // n_heads, 64 * n_heads), F32),
        compiler_params=_cparams(("arbitrary", "arbitrary")),
        name="mla_decode",
    )(page_table, qdec, knew, wuv_flat, cache_ckv, cache_krt)


def _diff_decode_kernel(layer, cp, n_chunks, n_heads, rep, n_new, lam_init, pt_ref, q_ref, knew_ref, vnew_ref,
                        bpast_ref, bnew_ref, lam_ref, gs_ref, kt_hbm, vt_hbm, o_ref, kt_buf, vt_buf, sems,
                        m_s, l_s, acc_s):
    chunk = pl.program_id(1)
    width = kt_buf.shape[1] * kt_buf.shape[2]
    ck = kt_buf.shape[3]
    page = ck // cp
    dsts = (lambda sl, j: kt_buf.at[sl, :, :, pl.ds(j * page, page)],
            lambda sl, j: vt_buf.at[sl, :, :, pl.ds(j * page, page)])
    slot = _chunk_pipeline(layer, cp, n_chunks, pt_ref, (kt_hbm, vt_hbm), dsts, sems)
    rows = q_ref.shape[1]
    half = rows // 2

    @pl.when(chunk == 0)
    def _():
        _flash_init(m_s, l_s, acc_s)

    q = q_ref[0]
    k_t = kt_buf[slot].reshape(width, ck).astype(BF16)
    v_t = vt_buf[slot].reshape(width, ck).astype(BF16)

    @pl.when(chunk < n_chunks - 1)
    def _():
        _online_update(_dot(q, k_t), lambda p: _nt_dot(p, v_t), m_s, l_s, acc_s)

    @pl.when(chunk == n_chunks - 1)
    def _():
        s = _dot(q, k_t)
        n_far = ck - BAND
        if n_far > 0:
            _online_update(s[:, 0:n_far], lambda p: _nt_dot(p, v_t[:, 0:n_far]), m_s, l_s, acc_s)
        _online_update(s[:, n_far:] + bpast_ref[...], lambda p: _nt_dot(p, v_t[:, n_far:]), m_s, l_s, acc_s)
        _online_update(_nt_dot(q, knew_ref[0]) + bnew_ref[...], lambda p: _dot(p, vnew_ref[0]), m_s, l_s, acc_s)
        lam = _lambda(lam_ref, lam_init)
        o = acc_s[...] * (1.0 / l_s[...])
        o = o[0:half] - lam * o[half:rows]
        hidx = lax.broadcasted_iota(jnp.int32, (half, 64), 0) % n_heads
        osel = o[:, 0:64]
        for g in range(1, n_heads // rep):
            osel = jnp.where(hidx // rep == g, o[:, 64 * g:64 * g + 64], osel)
        o_ref[0] = (_rms(osel, gs_ref[...]) * (1.0 - lam_init)).astype(o_ref.dtype)


def _diff_decode(layer, page_table, qdec, knew, vnew, bpast, bnew, lam_par, gsub, cache_kt, cache_vt,
                 n_heads, rep, n_new, lam_init):
    bd, rows, width = qdec.shape
    n_pages = page_table.shape[1]
    n_groups, dgrp, page = cache_kt.shape[2:]
    cp = min(DECODE_PAGES, n_pages)
    assert n_pages % cp == 0 and page == BAND and n_groups * dgrp == width
    n_chunks = n_pages // cp
    full2 = lambda a: pl.BlockSpec(a.shape, lambda b, c, pt: (0, 0))
    grid_spec = pltpu.PrefetchScalarGridSpec(
        num_scalar_prefetch=1,
        grid=(bd, n_chunks),
        in_specs=[pl.BlockSpec((1, rows, width), lambda b, c, pt: (b, 0, 0)),
                  pl.BlockSpec((1, 8, width), lambda b, c, pt: (b, 0, 0)),
                  pl.BlockSpec((1, 8, width), lambda b, c, pt: (b, 0, 0)),
                  full2(bpast), full2(bnew), full2(lam_par), full2(gsub),
                  pl.BlockSpec(memory_space=pl.ANY),
                  pl.BlockSpec(memory_space=pl.ANY)],
        out_specs=pl.BlockSpec((1, rows // 2, 64), lambda b, c, pt: (b, 0, 0)),
        scratch_shapes=[pltpu.VMEM((2, n_groups, dgrp, cp * page), F32),
                        pltpu.VMEM((2, n_groups, dgrp, cp * page), F32),
                        pltpu.SemaphoreType.DMA((2, 2)),
                        pltpu.VMEM((rows, 1), F32), pltpu.VMEM((rows, 1), F32), pltpu.VMEM((rows, width), F32)])
    return pl.pallas_call(
        functools.partial(_diff_decode_kernel, layer, cp, n_chunks, n_heads, rep, n_new, lam_init),
        grid_spec=grid_spec,
        out_shape=jax.ShapeDtypeStruct((bd, rows // 2, 64), BF16),
        compiler_params=_cparams(("arbitrary", "arbitrary")),
        name="diff_decode",
    )(page_table, qdec, knew, vnew, bpast, bnew, lam_par, gsub, cache_kt, cache_vt)


def _merge_kernel(x_ref, g1_ref, oa_ref, ub_ref, oc_ref, ga_ref, gb_ref, gc_ref,
                  woa_ref, wob_ref, woc_ref, wo_ref, o_ref):
    merged = (jax.nn.sigmoid(ga_ref[0]) * _dot(oa_ref[0], woa_ref[...])
              + jax.nn.sigmoid(gb_ref[0]) * _dot(ub_ref[0], wob_ref[...])
              + jax.nn.sigmoid(gc_ref[0]) * _dot(oc_ref[0], woc_ref[...]))
    o_ref[0] = x_ref[0] + g1_ref[0] * _dot(merged.astype(BF16), wo_ref[...])


def _merge(x, g1, oa, ub, oc, ga, gb, gc, lw):
    b, s, d = x.shape
    tm = min(TOKEN_TILE, s)
    tok = lambda width: pl.BlockSpec((1, tm, width), lambda bi, si: (bi, si, 0))
    consts = [lw[k] for k in ("woa", "wob", "woc", "wo")]
    return pl.pallas_call(
        _merge_kernel,
        grid=(b, s // tm),
        in_specs=[tok(d), _row_spec(g1, tm), tok(oa.shape[2]), tok(ub.shape[2]), tok(oc.shape[2]),
                  tok(d), tok(d), tok(d)] + [_const_spec(c.shape) for c in consts],
        out_specs=tok(d),
        out_shape=jax.ShapeDtypeStruct((b, s, d), F32),
        compiler_params=_cparams(("parallel", "parallel")),
        name="merge",
    )(x, g1, oa, ub, oc, ga, gb, gc, *consts)


def _ffn_kernel(seq_len, n_chunks, has_final, x_ref, halo_ref, sh_ref, sc_ref, g2_ref, gn_ref, p1_ref, p2_ref,
                wg_ref, wu_ref, wcv_ref, wd_ref, gf_ref, o_ref, u_ref):
    tm = x_ref.shape[1]
    ff = wg_ref.shape[1]
    tf = ff // n_chunks
    tu = u_ref.shape[1]
    x = x_ref[0]
    gn = gn_ref[...]
    h2 = (_rms(x, gn) * (1.0 + sc_ref[0]) + sh_ref[0]).astype(BF16)
    row = lax.broadcasted_iota(jnp.int32, (tm, 1), 0)
    if seq_len is None:
        sc_h = sc_ref[0][0:1] if sc_ref.shape[1] == 1 else sc_ref[0][0:8]
        sh_h = sh_ref[0][0:1] if sh_ref.shape[1] == 1 else sh_ref[0][0:8]
        live = (pl.program_id(1) > 0).astype(F32)
        hh = ((_rms(halo_ref[0], gn) * (1.0 + sc_h) + sh_h) * live).astype(BF16)
        pos = row
    else:
        pos = row % seq_len
    acc = jnp.zeros((tm, x.shape[1]), F32)
    for j in range(n_chunks):
        cols = slice(j * tf, (j + 1) * tf)
        u = _dot(h2, wg_ref[:, cols])
        if seq_len is None:
            uh = _dot(hh, wg_ref[:, cols])
            prev1 = uh[7:8]
            prev2 = jnp.where(row == 0, uh[6:7], uh[7:8])
        else:
            prev1 = p1_ref[0, :, cols]
            prev2 = p2_ref[0, :, cols]
        u1 = jnp.where(pos >= 1, pltpu.roll(u, 1, 0), prev1)
        u2 = jnp.where(pos >= 2, pltpu.roll(u, 2, 0), prev2)
        wcv = wcv_ref[:, cols]
        act = u2 * wcv[0:1] + u1 * wcv[1:2] + u * wcv[2:3]
        gated = (act * jax.nn.sigmoid(act)) * _dot(h2, wu_ref[:, cols])
        acc = acc + _dot(gated.astype(BF16), wd_ref[cols, :])
        u_ref[0, :, cols] = u[tm - tu:tm]
    y = x + g2_ref[0] * acc
    if has_final:
        y = _rms(y, gf_ref[...])
    o_ref[0] = y


def _ffn(x, sh, sc, g2, lw, prev, seq_len, g_final):
    b, s, d = x.shape
    ff = lw["wg"].shape[1]
    tm = min(TOKEN_TILE, s)
    n_chunks = 2 if (ff // 2) % LANES == 0 else 1
    tok = lambda width: pl.BlockSpec((1, tm, width), lambda bi, si: (bi, si, 0))
    if prev is None:
        tu = 8
        halo_spec = pl.BlockSpec((1, 8, d), lambda bi, si: (bi, jnp.maximum(si * (tm // 8) - 1, 0), 0))
        p1 = p2 = jnp.zeros((1, 8, LANES), F32)
        pspec = pl.BlockSpec((1, 8, LANES), lambda bi, si: (0, 0, 0))
    else:
        tu = tm
        halo_spec = pl.BlockSpec((1, 8, d), lambda bi, si: (bi, 0, 0))
        p1, p2 = prev
        pspec = tok(ff)
    has_final = g_final is not None
    gf = g_final.reshape(1, -1).astype(F32) if has_final else lw["gn2"]
    consts = [lw["wg"], lw["wu"], lw["wcv"], lw["wd"], gf]
    y, u = pl.pallas_call(
        functools.partial(_ffn_kernel, seq_len, n_chunks, has_final),
        grid=(b, s // tm),
        in_specs=[tok(d), halo_spec, _row_spec(sh, tm), _row_spec(sc, tm), _row_spec(g2, tm),
                  _const_spec(lw["gn2"].shape), pspec, pspec] + [_const_spec(c.shape) for c in consts],
        out_specs=[tok(d), pl.BlockSpec((1, tu, ff), lambda bi, si: (bi, si, 0))],
        out_shape=[jax.ShapeDtypeStruct((b, s, d), F32), jax.ShapeDtypeStruct((b, (s // tm) * tu, ff), F32)],
        compiler_params=_cparams(("parallel", "parallel")),
        name="ffn",
    )(x, x, sh, sc, g2, lw["gn2"], p1, p2, *consts)
    return y, u


def _rope_tables(pos, d_rope):
    half = d_rope // 2
    inv = ROPE_THETA ** (-jnp.arange(half, dtype=F32) / half)
    ang = pos.astype(F32)[:, None] * inv[None, :]
    cos, sin = jnp.cos(ang), jnp.sin(ang)
    pad = ((0, 0), (0, LANES - d_rope))
    c = jnp.pad(jnp.concatenate([cos, cos], axis=1), pad)
    sa = jnp.pad(jnp.concatenate([-sin, jnp.zeros_like(sin)], axis=1), pad)
    sb = jnp.pad(jnp.concatenate([jnp.zeros_like(sin), sin], axis=1), pad)
    return c, sa, sb


def _band_bias(rel_bias):
    n = jnp.arange(BAND, dtype=jnp.int32)
    max_exact = REL_BUCKETS // 2
    large = max_exact + (jnp.log(jnp.maximum(n, 1).astype(F32) / max_exact)
                         / math.log(REL_MAX_DIST / max_exact) * (REL_BUCKETS - max_exact)).astype(jnp.int32)
    bucket = jnp.where(n < max_exact, n, jnp.minimum(large, REL_BUCKETS - 1))
    tbl = rel_bias.astype(F32)
    return (tbl[bucket] - tbl[REL_BUCKETS - 1][None, :]).T


def _band_lookup(band, dist):
    val = band[:, jnp.clip(dist, 0, BAND - 1)]
    val = jnp.where(dist[None] >= BAND, 0.0, val)
    return jnp.where(dist[None] < 0, NEG, val)


def _layer_weights(l, p, sizes):
    offs = np.concatenate([[0], np.cumsum(sizes)])
    w = p["w_in"][l]
    seg = [w[:, offs[i]:offs[i + 1]].astype(BF16) for i in range(12)]
    d_rope = sizes[2]
    w_uq = p["w_uq"][l]
    d_nope = w_uq.shape[2] - d_rope
    row = lambda a: a.reshape(1, -1).astype(F32)
    lw = {
        "wq": seg[0], "wkv": seg[1], "wkr": jnp.pad(seg[2], ((0, 0), (0, LANES - d_rope))),
        "wb": seg[3], "wc": seg[4], "wx": seg[5], "wdq": seg[6], "wdk": seg[7], "wdv": seg[8],
        "wga": seg[9], "wgb": seg[10], "wgc": seg[11],
        "gn1": row(p["g_norm1"][l]), "gcq": row(p["g_cq"][l]), "gckv": row(p["g_ckv"][l]),
        "wuqn": jnp.transpose(w_uq[:, :, :d_nope], (1, 0, 2)).astype(BF16),
        "wuqr": jnp.pad(jnp.transpose(w_uq[:, :, d_nope:], (1, 0, 2)),
                        ((0, 0), (0, 0), (0, LANES - d_rope))).astype(BF16),
        "wukt": jnp.transpose(p["w_uk"][l], (1, 2, 0)).astype(BF16),
        "wuvt": jnp.transpose(p["w_uv"][l], (1, 2, 0)).astype(BF16),
        "wuv_flat": p["w_uv"][l].reshape(p["w_uv"].shape[1], -1).astype(BF16),
        "woa": p["w_oa"][l].astype(BF16), "wob": p["w_ob"][l].astype(BF16), "woc": p["w_oc"][l].astype(BF16),
        "wo": p["w_o"][l].astype(BF16),
        "wsc": p["w_sc"][l].astype(F32),
        "lam": jnp.stack([p["lam_q1"][l], p["lam_k1"][l], p["lam_q2"][l], p["lam_k2"][l]]).astype(F32),
        "gsub": row(p["g_subln"][l]),
        "gsub_col": jnp.broadcast_to(p["g_subln"][l].astype(F32)[:, None], (p["g_subln"].shape[1], DIFF_TILE)),
        "gn2": row(p["g_norm2"][l]),
        "wg": p["w_gate"][l].astype(BF16), "wu": p["w_up"][l].astype(BF16),
        "wcv": p["w_ffn_conv"][l].astype(F32), "wd": p["w_down"][l].astype(BF16),
        "mla_scale": float((d_nope + d_rope) ** -0.5 * LOG2E),
        "diff_scale": float((sizes[6] // w_uq.shape[1] // 2) ** -0.5 * LOG2E),
    }
    return lw


def _short_conv(v, prev, w):
    ext = jnp.concatenate([prev.astype(v.dtype), v], axis=1)
    s = v.shape[1]
    y = ext[:, 0:s] * w[0] + ext[:, 1:s + 1] * w[1] + ext[:, 2:s + 2] * w[2]
    return y, ext[:, s:s + 2]


def kernel(x_prompt, x_sample, c_prompt, c_sample, cache_mla_ckv, cache_mla_krope, cache_diff_k, cache_diff_v, state_shortconv, state_ffn_conv, page_table, rel_bias, w_ada, b_ada, g_norm1, w_in, g_cq, g_ckv, w_uq, w_uk, w_uv, w_oa, w_sc, w_ob, lam_q1, lam_k1, lam_q2, lam_k2, g_subln, w_oc, w_o, g_norm2, w_gate, w_up, w_ffn_conv, w_down, g_final):
    p = dict(w_in=w_in, g_norm1=g_norm1, g_cq=g_cq, g_ckv=g_ckv, w_uq=w_uq, w_uk=w_uk, w_uv=w_uv, w_oa=w_oa,
             w_sc=w_sc, w_ob=w_ob, lam_q1=lam_q1, lam_k1=lam_k1, lam_q2=lam_q2, lam_k2=lam_k2, g_subln=g_subln,
             w_oc=w_oc, w_o=w_o, g_norm2=g_norm2, w_gate=w_gate, w_up=w_up, w_ffn_conv=w_ffn_conv, w_down=w_down)
    n_layers = w_in.shape[0]
    bp, sp, d = x_prompt.shape
    bs, ss, _ = x_sample.shape
    n_heads = w_uq.shape[2]
    d_rope = cache_mla_krope.shape[-1]
    n_groups = cache_diff_k.shape[3]
    rep = n_heads // n_groups
    page = cache_mla_ckv.shape[2]
    past_len = page_table.shape[1] * page
    sizes = (w_uq.shape[1], cache_mla_ckv.shape[-1], d_rope, w_sc.shape[2], w_sc.shape[2], w_sc.shape[2],
             w_oc.shape[1], n_groups * cache_diff_k.shape[-1], n_groups * cache_diff_v.shape[-1], d, d, d)
    assert sum(sizes) == w_in.shape[2]
    page_table = page_table.astype(jnp.int32)

    n_c = bp + bs
    c_all = jnp.pad(jnp.concatenate([c_prompt, c_sample], axis=0), ((0, (-n_c) % 8), (0, 0)))
    mod = _adaln(c_all, w_ada, b_ada)

    tabs_p = _rope_tables(jnp.arange(sp, dtype=jnp.int32), d_rope)
    tabs_s = _rope_tables(past_len + (jnp.arange(bs * ss, dtype=jnp.int32) % ss), d_rope)
    band = _band_bias(rel_bias) * LOG2E
    ar = jnp.arange(BAND, dtype=jnp.int32)
    dtile_t = _band_lookup(band, ar[None, :] - ar[:, None])
    ptile_t = _band_lookup(band, BAND + ar[None, :] - ar[:, None])
    qi_rows = jnp.repeat(jnp.arange(ss, dtype=jnp.int32), n_heads)
    bpast = _band_lookup(band, BAND + qi_rows[:, None] - ar[None, :])
    new_idx = jnp.arange(8, dtype=jnp.int32)
    dist_new = jnp.where(new_idx[None, :] < ss, qi_rows[:, None] - new_idx[None, :], -1)
    bnew = _band_lookup(band, dist_new)
    hrow = jnp.tile(jnp.arange(n_heads), ss)
    pick = lambda a: jnp.tile(a[hrow, jnp.arange(ss * n_heads)], (2, 1))
    bpast, bnew = pick(bpast), pick(bnew)

    cache_krt = jnp.transpose(cache_mla_krope, (0, 1, 3, 2))
    cache_dkt = jnp.transpose(cache_diff_k, (0, 1, 3, 4, 2))
    cache_dvt = jnp.transpose(cache_diff_v, (0, 1, 3, 4, 2))

    xp, xs = x_prompt, x_sample.reshape(1, bs * ss, d)
    p_new, s_new = [], []
    for l in range(n_layers):
        lw = _layer_weights(l, p, sizes)
        lam_init = 0.8 - 0.6 * math.exp(-0.3 * l)
        mods_p = [m[:, None, :] for m in jnp.split(mod[l, :bp], 6, axis=-1)]
        mods_s = [jnp.repeat(m, ss, axis=0)[None] for m in jnp.split(mod[l, bp:n_c], 6, axis=-1)]

        sh1, sc1, g1, sh2, sc2, g2 = mods_p
        (qcat, ckv, kr, kcat, ckvt, zb, v, qd1, qd2, dk, dv, kd, vdt, ga, gb, gc) = _in_proj(xp, sh1, sc1, lw, tabs_p)
        oa = _mla_prefill(qcat, kcat, ckvt, lw["wuvt"])
        oc = _diff_prefill(qd1, qd2, kd, vdt, dtile_t, ptile_t, lw["lam"], lw["gsub_col"], lam_init)
        y_sc, sc_state = _short_conv(v, jnp.zeros((bp, 2, v.shape[2]), F32), lw["wsc"])
        xp = _merge(xp, g1, oa, (zb * y_sc).astype(BF16), oc, ga, gb, gc, lw)
        xp, u_tail = _ffn(xp, sh2, sc2, g2, lw, None, None, g_final if l == n_layers - 1 else None)
        p_new.append((ckv, kr, dk.reshape(bp, sp, n_groups, -1), dv.reshape(bp, sp, n_groups, -1),
                      sc_state, u_tail[:, -2:]))

        sh1, sc1, g1, sh2, sc2, g2 = mods_s
        (qcat, ckv, kr, kcat, ckvt, zb, v, qd1, qd2, dk, dv, kd, vdt, ga, gb, gc) = _in_proj(xs, sh1, sc1, lw, tabs_s)
        pad_new = lambda a: jnp.pad(a.reshape(bs, ss, a.shape[-1]), ((0, 0), (0, 8 - ss), (0, 0)))
        qdec = jnp.transpose(qcat[0].reshape(n_heads, bs, ss, 384), (1, 2, 0, 3)).reshape(bs, ss * n_heads, 384)
        oa = _mla_decode(l, page_table, qdec, pad_new(kcat[0]), lw["wuv_flat"], cache_mla_ckv, cache_krt,
                         n_heads, ss)
        def dec_rows(qd):
            a = jnp.transpose(qd[0].reshape(n_groups, rep, bs, ss, 64), (2, 3, 0, 1, 4))
            a = a[:, :, :, :, None, :] * jnp.eye(n_groups, dtype=a.dtype)[None, None, :, None, :, None]
            return a.reshape(bs, ss * n_heads, n_groups * 64)
        qd_dec = jnp.concatenate([dec_rows(qd1), dec_rows(qd2)], axis=1)
        oc = _diff_decode(l, page_table, qd_dec, pad_new(dk[0].astype(BF16)), pad_new(dv[0].astype(BF16)),
                          bpast, bnew, lw["lam"], lw["gsub"], cache_dkt, cache_dvt, n_heads, rep, ss, lam_init)
        oa = oa.reshape(1, bs * ss, -1).astype(BF16)
        oc = oc.reshape(1, bs * ss, -1)
        y_sc, sc_state = _short_conv(v.reshape(bs, ss, -1), state_shortconv[l], lw["wsc"])
        ub = (zb * y_sc.reshape(1, bs * ss, -1)).astype(BF16)
        xs = _merge(xs, g1, oa, ub, oc, ga, gb, gc, lw)
        st = state_ffn_conv[l].astype(F32)
        zero = jnp.zeros((bs, ss, st.shape[2]), F32)
        p1 = zero.at[:, 0].set(st[:, 1]).reshape(1, bs * ss, -1)
        p2 = zero.at[:, 0].set(st[:, 0]).at[:, 1].set(st[:, 1]).reshape(1, bs * ss, -1)
        xs, u_all = _ffn(xs, sh2, sc2, g2, lw, (p1, p2), ss, g_final if l == n_layers - 1 else None)
        s_new.append((ckv.reshape(bs, ss, -1), kr.reshape(bs, ss, -1), dk.reshape(bs, ss, n_groups, -1),
                      dv.reshape(bs, ss, n_groups, -1), sc_state, u_all.reshape(bs, ss, -1)[:, ss - 2:]))

    stack = lambda lst, i: jnp.stack([e[i] for e in lst], axis=0)
    return ((xp, xs.reshape(bs, ss, d)) + tuple(stack(p_new, i) for i in range(6))
            + tuple(stack(s_new, i) for i in range(6)))
```
